```python
import math
import jax, jax.numpy as jnp
from jax import lax
import numpy as np

D_MODEL = 1024
BATCH = 32
SEQ = 256
DEPTH = 2
DEC_BATCH = 4
DEC_SEQ = 1024
PAST_LEN = 256

GRID_W = 64
HEAD_DIM = 64
ATTN_HEADS = 8
ATTN_KV_HEADS = 2
ATTN_GROUP = ATTN_HEADS // ATTN_KV_HEADS
DIFF_HEADS = 4
DIFF_QK_DIM = 2 * HEAD_DIM
DIFF_V_DIM = 2 * HEAD_DIM
RWKV_HEADS = 8
RWKV_HEAD_DIM = 64
BRANCH_WIDTH = 512
W_LORA = 64
A_LORA = 64
G_LORA = 128
D_FF = 2816
Q_BLOCK = 128
N_BRANCH = 3
N_MOD = 9
ROPE_THETA = 10000.0
ROPE_AXIS_DIM = HEAD_DIM // 2
DEEPNORM_ALPHA = (2.0 * DEPTH) ** 0.25
DEEPNORM_BETA = (8.0 * DEPTH) ** -0.25
LN_EPS = 1e-5
RMS_EPS = 1e-6
RWKV_GN_EPS = 64e-5
A_Q_W = ATTN_HEADS * HEAD_DIM
A_KV_W = ATTN_KV_HEADS * HEAD_DIM
B_QK_W = DIFF_HEADS * DIFF_QK_DIM
B_V_W = DIFF_HEADS * DIFF_V_DIM
C_IN_W = 3 * BRANCH_WIDTH + 2 * W_LORA + 2 * A_LORA + G_LORA
IN_SPLITS = (N_BRANCH * D_MODEL, A_Q_W, A_KV_W, A_KV_W, B_QK_W, B_QK_W, B_V_W, C_IN_W)
IN_WIDTH = sum(IN_SPLITS)
C_SPLITS = (BRANCH_WIDTH, BRANCH_WIDTH, BRANCH_WIDTH, 2 * W_LORA, 2 * A_LORA, G_LORA)

kernel_name = 'hybrid_flow_prefix_trunk'


def _split(x, sizes):
    return jnp.split(x, np.cumsum(sizes)[:-1].tolist(), axis=-1)


def layer_norm(x, g, b):
    xf = x.astype(jnp.float32)
    mu = jnp.mean(xf, -1, keepdims=True)
    var = jnp.mean(jnp.square(xf - mu), -1, keepdims=True)
    return ((xf - mu) * lax.rsqrt(var + LN_EPS) * g + b).astype(x.dtype)


def rms_norm(x, g):
    xf = x.astype(jnp.float32)
    return (xf * lax.rsqrt(jnp.mean(xf * xf, -1, keepdims=True) + RMS_EPS) * g).astype(x.dtype)


def axial_rope(T):
    n_rows = T // GRID_W
    row = jnp.repeat(jnp.arange(n_rows), GRID_W).astype(jnp.float32)
    col = jnp.tile(jnp.arange(GRID_W), n_rows).astype(jnp.float32)
    inv = 1.0 / (ROPE_THETA ** (jnp.arange(0, ROPE_AXIS_DIM, 2, dtype=jnp.float32) / ROPE_AXIS_DIM))
    ang = jnp.stack([row[:, None] * inv, col[:, None] * inv], axis=1)
    return jnp.cos(ang), jnp.sin(ang)


def apply_rope(x, cos, sin):
    shp = x.shape
    xr = x.astype(jnp.float32).reshape(shp[:-1] + (2, 2, ROPE_AXIS_DIM // 2))
    x1, x2 = xr[..., 0, :], xr[..., 1, :]
    out = jnp.stack([x1 * cos - x2 * sin, x2 * cos + x1 * sin], axis=-2)
    return out.reshape(shp).astype(x.dtype)


def map_query_blocks(fn, *qs):
    Tq = qs[0].shape[-2]
    nb = Tq // Q_BLOCK
    blocks = tuple(jnp.moveaxis(q.reshape(q.shape[:-2] + (nb, Q_BLOCK, q.shape[-1])), -3, 0) for q in qs)
    out = lax.map(lambda b: fn(*b), blocks)
    out = jnp.moveaxis(out, 0, -3)
    return out.reshape(out.shape[:-3] + (Tq, out.shape[-1]))


def _gqa_block(q, k, v):
    s = jnp.einsum('bhgqd,bhkd->bhgqk', q, k).astype(jnp.float32) * HEAD_DIM ** -0.5
    p = jax.nn.softmax(s, axis=-1).astype(v.dtype)
    return jnp.einsum('bhgqk,bhkd->bhgqd', p, v)


def _diff_block(q1, q2, k1, k2, v, lam):
    sc = HEAD_DIM ** -0.5
    s1 = jnp.einsum('bhqd,bhkd->bhqk', q1, k1).astype(jnp.float32) * sc
    s2 = jnp.einsum('bhqd,bhkd->bhqk', q2, k2).astype(jnp.float32) * sc
    p = jax.nn.softmax(s1, axis=-1) - lam * jax.nn.softmax(s2, axis=-1)
    return jnp.einsum('bhqk,bhkd->bhqd', p.astype(v.dtype), v)


def swiglu(x, w_in, w_out):
    a, b = jnp.split(x @ w_in, 2, axis=-1)
    return (jax.nn.silu(a) * b) @ w_out


def _rwkv_scan(S0, r, w, kk, a, k, v, reverse):
    def step(S, inp):
        r_t, w_t, kk_t, a_t, k_t, v_t = inp
        sa = jnp.einsum('bhvk,bhk->bhv', S, -kk_t)
        S = S * w_t[:, :, None, :] + sa[..., None] * (kk_t * a_t)[:, :, None, :] + v_t[..., None] * k_t[:, :, None, :]
        return S, jnp.einsum('bhvk,bhk->bhv', S, r_t)
    xs = tuple(jnp.moveaxis(t, 1, 0) for t in (r, w, kk, a, k, v))
    S, out = lax.scan(step, S0, xs, reverse=reverse)
    return S, jnp.moveaxis(out, 0, 1)


def rwkv7_bidir(cin, p, state0):
    B, T, _ = cin.shape
    f32 = jnp.float32
    prev = jnp.pad(cin[:, :-1], ((0, 0), (1, 0), (0, 0)))
    nxt = jnp.pad(cin[:, 1:], ((0, 0), (0, 1), (0, 0)))
    cin = cin + p['rwkv_mu'] * (0.5 * (prev + nxt) - cin)
    r, k, v, wl, al, gl = _split(cin, C_SPLITS)
    wl = wl.reshape(B, T, 2, W_LORA)
    al = al.reshape(B, T, 2, A_LORA)
    w_logit = (p['rwkv_w0'] + jnp.einsum('btdr,drc->btdc', jnp.tanh(wl), p['rwkv_w2'])).astype(f32)
    decay = jnp.exp(-jnp.exp(-jax.nn.softplus(-w_logit) - 0.5))
    a = jax.nn.sigmoid((p['rwkv_a0'] + jnp.einsum('btdr,drc->btdc', al, p['rwkv_a2'])).astype(f32))
    g = jax.nn.sigmoid(gl) @ p['rwkv_g2']
    heads = lambda t: t.reshape(t.shape[:-1] + (RWKV_HEADS, RWKV_HEAD_DIM))
    kk = heads(k * p['rwkv_k_k']).astype(f32)
    kk = kk * lax.rsqrt(jnp.sum(kk * kk, -1, keepdims=True) + 1e-12)
    k_h = heads(k[:, :, None, :] * (1.0 + (a - 1.0) * p['rwkv_k_a'])).astype(f32)
    a_h, w_h = heads(a), heads(decay)
    rh, vh = heads(r).astype(f32), heads(v).astype(f32)
    outs, states = [], []
    for d in range(2):
        S, o = _rwkv_scan(state0[:, d].astype(f32), rh, w_h[:, :, d], kk, a_h[:, :, d], k_h[:, :, d], vh, reverse=(d == 1))
        outs.append(o)
        states.append(S)
    o = outs[0] + outs[1]
    mu = jnp.mean(o, -1, keepdims=True)
    var = jnp.mean(jnp.square(o - mu), -1, keepdims=True)
    o = ((o - mu) * lax.rsqrt(var + RWKV_GN_EPS)).reshape(B, T, BRANCH_WIDTH) * p['rwkv_gn_g'] + p['rwkv_gn_b']
    bonus = jnp.sum(rh[:, :, None] * k_h * p['rwkv_r_k'], -1, keepdims=True) * vh[:, :, None]
    o = o + jnp.sum(bonus, axis=2).reshape(B, T, BRANCH_WIDTH)
    return (o * g).astype(cin.dtype), jnp.stack(states, axis=1)


def mixing_sublayer(h, p, lam_init, ctx):
    B, T, _ = h.shape
    gates, aq, ak, av, bq, bk, bv, cin = _split(h @ p['w_in'], IN_SPLITS)
    latent = ctx is not None
    if latent:
        cos, sin = axial_rope(T)
        rope = lambda t: apply_rope(t, cos, sin)
    else:
        rope = lambda t: t
    qa = rms_norm(aq.reshape(B, T, ATTN_KV_HEADS, ATTN_GROUP, HEAD_DIM), p['qk_norm_g'][0]).transpose(0, 2, 3, 1, 4)
    ka = rms_norm(ak.reshape(B, T, ATTN_KV_HEADS, HEAD_DIM), p['qk_norm_g'][1]).transpose(0, 2, 1, 3)
    va = av.reshape(B, T, ATTN_KV_HEADS, HEAD_DIM).transpose(0, 2, 1, 3)
    qb = bq.reshape(B, T, DIFF_HEADS, DIFF_QK_DIM).transpose(0, 2, 1, 3)
    kb = bk.reshape(B, T, DIFF_HEADS, DIFF_QK_DIM).transpose(0, 2, 1, 3)
    vb = bv.reshape(B, T, DIFF_HEADS, DIFF_V_DIM).transpose(0, 2, 1, 3)
    q1, q2 = rope(qb[..., :HEAD_DIM]), rope(qb[..., HEAD_DIM:])
    if latent:
        ctx_ka, ctx_va, ctx_kb, ctx_vb, state0 = ctx
        keys_a = jnp.concatenate([rope(ka), ctx_ka], axis=2)
        vals_a = jnp.concatenate([va, ctx_va], axis=2)
        k1 = jnp.concatenate([rope(kb[..., :HEAD_DIM]), ctx_kb[..., :HEAD_DIM]], axis=2)
        k2 = jnp.concatenate([rope(kb[..., HEAD_DIM:]), ctx_kb[..., HEAD_DIM:]], axis=2)
        vals_b = jnp.concatenate([vb, ctx_vb], axis=2)
    else:
        keys_a, vals_a = ka, va
        k1, k2 = kb[..., :HEAD_DIM], kb[..., HEAD_DIM:]
        vals_b = vb
        state0 = jnp.zeros((B, 2, RWKV_HEADS, RWKV_HEAD_DIM, RWKV_HEAD_DIM), jnp.float32)
    out_a = map_query_blocks(lambda q: _gqa_block(q, keys_a, vals_a), rope(qa))
    out_a = out_a.transpose(0, 3, 1, 2, 4).reshape(B, T, A_Q_W)
    lv = p['diff_lambda'].astype(jnp.float32)
    lam = jnp.exp(jnp.dot(lv[0], lv[1])) - jnp.exp(jnp.dot(lv[2], lv[3])) + lam_init
    out_b = map_query_blocks(lambda x1, x2: _diff_block(x1, x2, k1, k2, vals_b, lam), q1, q2)
    out_b = rms_norm(out_b, p['diff_subln_g']) * (1.0 - lam_init)
    out_b = out_b.transpose(0, 2, 1, 3).reshape(B, T, B_V_W)
    out_c, state = rwkv7_bidir(cin, p, state0)
    branches = jnp.stack([out_a, out_b, out_c], axis=2)
    up = jnp.einsum('btim,imd->btid', branches, p['w_up'])
    gate = jax.nn.sigmoid(gates.reshape(B, T, N_BRANCH, D_MODEL))
    out = jnp.sum(gate * up, axis=2) @ p['w_out']
    return out, (ka, va, kb, vb, state.astype(h.dtype))


def modulation(cvec, p):
    m = jax.nn.silu(cvec) @ p['w_mod'] + p['b_mod']
    return m.reshape(cvec.shape[0], N_MOD, D_MODEL)


def trunk_layer(x, mod, p, lam_init, ctx):
    m = lambda i: mod[:, None, i]
    adaln = lambda t, s: t * (1.0 + m(3 * s + 1)) + m(3 * s)
    h = swiglu(adaln(x, 0), p['ffn_w_in'][0], p['ffn_w_out'][0])
    x = layer_norm(DEEPNORM_ALPHA * x + 0.5 * m(2) * h, p['ln_g'][0], p['ln_b'][0])
    h, ctx_out = mixing_sublayer(adaln(x, 1), p, lam_init, ctx)
    x = layer_norm(DEEPNORM_ALPHA * x + m(5) * h, p['ln_g'][1], p['ln_b'][1])
    h = swiglu(adaln(x, 2), p['ffn_w_in'][1], p['ffn_w_out'][1])
    x = layer_norm(DEEPNORM_ALPHA * x + 0.5 * m(8) * h, p['ln_g'][2], p['ln_b'][2])
    return x, ctx_out


def setup_inputs(seed: int = 0) -> dict:
    key = jax.random.key(seed)
    ks = iter(jax.random.split(key, 40))
    nrm = lambda shape, s: jax.random.normal(next(ks), shape, jnp.float32) * s
    L, D = DEPTH, D_MODEL
    return {
        'x_prompt': nrm((BATCH, SEQ, D), 1.0),
        'x_sample': nrm((DEC_BATCH, DEC_SEQ, D), 1.0),
        'cache_attn_k': nrm((DEC_BATCH, L, ATTN_KV_HEADS, PAST_LEN, HEAD_DIM), 1.0),
        'cache_attn_v': nrm((DEC_BATCH, L, ATTN_KV_HEADS, PAST_LEN, HEAD_DIM), 1.0),
        'cache_diff_k': nrm((DEC_BATCH, L, DIFF_HEADS, PAST_LEN, DIFF_QK_DIM), 1.0),
        'cache_diff_v': nrm((DEC_BATCH, L, DIFF_HEADS, PAST_LEN, DIFF_V_DIM), 1.0),
        'state_rwkv': nrm((DEC_BATCH, L, 2, RWKV_HEADS, RWKV_HEAD_DIM, RWKV_HEAD_DIM), 0.3),
        'c': nrm((DEC_BATCH, D), 1.0),
        'c_ctx': nrm((D,), 1.0),
        'w_mod': nrm((L, D, N_MOD * D), 0.5 * D ** -0.5),
        'b_mod': nrm((L, N_MOD * D), 0.02),
        'ln_g': 1.0 + nrm((L, 3, D), 0.02),
        'ln_b': nrm((L, 3, D), 0.02),
        'ffn_w_in': nrm((L, 2, D, 2 * D_FF), D ** -0.5),
        'ffn_w_out': nrm((L, 2, D_FF, D), DEEPNORM_BETA * D_FF ** -0.5),
        'w_in': nrm((L, D, IN_WIDTH), D ** -0.5),
        'qk_norm_g': 1.0 + nrm((L, 2, HEAD_DIM), 0.02),
        'diff_lambda': nrm((L, 4, HEAD_DIM), 0.1),
        'diff_subln_g': 1.0 + nrm((L, DIFF_V_DIM), 0.02),
        'rwkv_mu': jax.random.uniform(next(ks), (L, C_IN_W), jnp.float32),
        'rwkv_w0': -2.0 + nrm((L, 2, BRANCH_WIDTH), 0.5),
        'rwkv_w2': nrm((L, 2, W_LORA, BRANCH_WIDTH), 0.1),
        'rwkv_a0': nrm((L, 2, BRANCH_WIDTH), 0.1),
        'rwkv_a2': nrm((L, 2, A_LORA, BRANCH_WIDTH), A_LORA ** -0.5),
        'rwkv_g2': nrm((L, G_LORA, BRANCH_WIDTH), G_LORA ** -0.5),
        'rwkv_k_k': 1.0 + nrm((L, BRANCH_WIDTH), 0.1),
        'rwkv_k_a': 1.0 + nrm((L, BRANCH_WIDTH), 0.1),
        'rwkv_r_k': nrm((L, RWKV_HEADS, RWKV_HEAD_DIM), 0.1),
        'rwkv_gn_g': 1.0 + nrm((L, BRANCH_WIDTH), 0.02),
        'rwkv_gn_b': nrm((L, BRANCH_WIDTH), 0.02),
        'w_up': nrm((L, N_BRANCH, BRANCH_WIDTH, D), BRANCH_WIDTH ** -0.5),
        'w_out': nrm((L, D, D), DEEPNORM_BETA * D ** -0.5),
    }


def reference(x_prompt, x_sample, cache_attn_k, cache_attn_v, cache_diff_k, cache_diff_v, state_rwkv, c, c_ctx,
              w_mod, b_mod, ln_g, ln_b, ffn_w_in, ffn_w_out, w_in, qk_norm_g, diff_lambda, diff_subln_g,
              rwkv_mu, rwkv_w0, rwkv_w2, rwkv_a0, rwkv_a2, rwkv_g2, rwkv_k_k, rwkv_k_a, rwkv_r_k,
              rwkv_gn_g, rwkv_gn_b, w_up, w_out):
    y_prompt, y_sample = x_prompt, x_sample
    n_ak, n_av, n_dk, n_dv, n_st = [], [], [], [], []
    for l in range(DEPTH):
        p = {'w_mod': w_mod[l], 'b_mod': b_mod[l], 'ln_g': ln_g[l], 'ln_b': ln_b[l],
             'ffn_w_in': ffn_w_in[l], 'ffn_w_out': ffn_w_out[l], 'w_in': w_in[l], 'qk_norm_g': qk_norm_g[l],
             'diff_lambda': diff_lambda[l], 'diff_subln_g': diff_subln_g[l], 'rwkv_mu': rwkv_mu[l],
             'rwkv_w0': rwkv_w0[l], 'rwkv_w2': rwkv_w2[l], 'rwkv_a0': rwkv_a0[l], 'rwkv_a2': rwkv_a2[l],
             'rwkv_g2': rwkv_g2[l], 'rwkv_k_k': rwkv_k_k[l], 'rwkv_k_a': rwkv_k_a[l], 'rwkv_r_k': rwkv_r_k[l],
             'rwkv_gn_g': rwkv_gn_g[l], 'rwkv_gn_b': rwkv_gn_b[l], 'w_up': w_up[l], 'w_out': w_out[l]}
        lam_init = 0.8 - 0.6 * math.exp(-0.3 * l)
        y_prompt, (ka, va, kb, vb, st) = trunk_layer(y_prompt, modulation(c_ctx[None], p), p, lam_init, None)
        n_ak.append(ka); n_av.append(va); n_dk.append(kb); n_dv.append(vb); n_st.append(st)
        ctx = (cache_attn_k[:, l], cache_attn_v[:, l], cache_diff_k[:, l], cache_diff_v[:, l], state_rwkv[:, l])
        y_sample, _ = trunk_layer(y_sample, modulation(c, p), p, lam_init, ctx)
    new_attn_k = jnp.stack(n_ak, axis=1)
    new_attn_v = jnp.stack(n_av, axis=1)
    new_diff_k = jnp.stack(n_dk, axis=1)
    new_diff_v = jnp.stack(n_dv, axis=1)
    new_state_rwkv = jnp.stack(n_st, axis=1)
    return (y_prompt, y_sample, new_attn_k, new_attn_v, new_diff_k, new_diff_v, new_state_rwkv)
```

```python
import functools
import math

import numpy as np
import jax
import jax.numpy as jnp
from jax import lax
from jax.experimental import pallas as pl
from jax.experimental.pallas import tpu as pltpu

F32 = jnp.float32
BF16 = jnp.bfloat16

D_MODEL = 1024
GRID_W = 64
HEAD_DIM = 64
ATTN_HEADS = 8
ATTN_KV_HEADS = 2
ATTN_GROUP = ATTN_HEADS // ATTN_KV_HEADS
DIFF_HEADS = 4
RWKV_HEADS = 8
RWKV_HEAD_DIM = 64
BRANCH_WIDTH = 512
W_LORA = 64
A_LORA = 64
G_LORA = 128
D_FF = 2816
N_BRANCH = 3
N_MOD = 9
ROPE_THETA = 10000.0
ROPE_AXIS_DIM = HEAD_DIM // 2
LN_EPS = 1e-5
RMS_EPS = 1e-6
RWKV_GN_EPS = 64e-5
A_Q_W = ATTN_HEADS * HEAD_DIM
A_KV_W = ATTN_KV_HEADS * HEAD_DIM
B_QK_W = DIFF_HEADS * 2 * HEAD_DIM
B_V_W = DIFF_HEADS * 2 * HEAD_DIM
C_IN_W = 3 * BRANCH_WIDTH + 2 * W_LORA + 2 * A_LORA + G_LORA
IN_SPLITS = (N_BRANCH * D_MODEL, A_Q_W, A_KV_W, A_KV_W, B_QK_W, B_QK_W, B_V_W, C_IN_W)
IN_WIDTH = sum(IN_SPLITS)

LANES = 128
SUBLANES = 8
VMEM_LIMIT_BYTES = 56 * 1024 * 1024

MOD_ROWS = 8
MOD_TN = 1152
FFN_CK = 256
ROW_TILE = 256
Q_TILE = 256
SCAN_TC = 32


def _cparams(*sem):
    return pltpu.CompilerParams(dimension_semantics=sem, vmem_limit_bytes=VMEM_LIMIT_BYTES)


def _sigmoid(x):
    return 1.0 / (1.0 + jnp.exp(-x))


def _dot(a, b):
    return jnp.dot(a, b, preferred_element_type=F32)


def _dot_t(a, b):
    return lax.dot_general(a, b, (((1,), (1,)), ((), ())), preferred_element_type=F32)


def _seg_sum(x, ones_bd):
    hi = x.astype(BF16)
    r1 = x - hi.astype(F32)
    mid = r1.astype(BF16)
    lo = (r1 - mid.astype(F32)).astype(BF16)
    return _dot(hi, ones_bd) + _dot(mid, ones_bd) + _dot(lo, ones_bd)


def _layer_norm(y, g, b):
    mu = jnp.mean(y, axis=-1, keepdims=True)
    d = y - mu
    var = jnp.mean(d * d, axis=-1, keepdims=True)
    return d * lax.rsqrt(var + LN_EPS) * g + b


def _mod_row_map(n_prompt_blocks, blocks_per_sample):
    def row(i):
        return jnp.where(i < n_prompt_blocks, 0, 1 + (i - n_prompt_blocks) // blocks_per_sample)
    return row


def _mod_kernel(c_ref, w_ref, b_ref, o_ref):
    cv = c_ref[...]
    h = (cv * _sigmoid(cv)).astype(BF16)
    o_ref[0] = _dot(h, w_ref[0].astype(BF16)) + b_ref[0]


def _modulation(cvecs, w_mod, b_mod):
    L = w_mod.shape[0]
    n = N_MOD * D_MODEL
    out = pl.pallas_call(
        _mod_kernel,
        out_shape=jax.ShapeDtypeStruct((L, MOD_ROWS, n), F32),
        grid=(L, n // MOD_TN),
        in_specs=[
            pl.BlockSpec((MOD_ROWS, D_MODEL), lambda l, j: (0, 0)),
            pl.BlockSpec((1, D_MODEL, MOD_TN), lambda l, j: (l, 0, j)),
            pl.BlockSpec((1, 1, MOD_TN), lambda l, j: (l, 0, j)),
        ],
        out_specs=pl.BlockSpec((1, MOD_ROWS, MOD_TN), lambda l, j: (l, 0, j)),
        compiler_params=_cparams("arbitrary", "arbitrary"),
        name="modulation",
    )(cvecs, w_mod, b_mod.reshape(L, 1, n))
    return out.reshape(L, MOD_ROWS, N_MOD, D_MODEL)


def _ffn_kernel(x_ref, mod_ref, win_ref, wout_ref, g_ref, b_ref, o_ref, act_ref, *, sub, alpha):
    x = x_ref[...]
    shift = mod_ref[0, 3 * sub:3 * sub + 1, :]
    scale = mod_ref[0, 3 * sub + 1:3 * sub + 2, :]
    gate = mod_ref[0, 3 * sub + 2:3 * sub + 3, :]
    h = (x * (1.0 + scale) + shift).astype(BF16)
    for c in range(D_FF // FFN_CK):
        a = _dot(h, win_ref[:, c * FFN_CK:(c + 1) * FFN_CK])
        b = _dot(h, win_ref[:, D_FF + c * FFN_CK:D_FF + (c + 1) * FFN_CK])
        act_ref[:, c * FFN_CK:(c + 1) * FFN_CK] = (a * _sigmoid(a) * b).astype(BF16)
    f = _dot(act_ref[...], wout_ref[...])
    y = alpha * x + (0.5 * gate) * f
    o_ref[...] = _layer_norm(y, g_ref[...], b_ref[...])


def _ffn_sublayer(x, mod, w_in, w_out, ln_g, ln_b, *, sub, alpha, row_map):
    n = x.shape[0]
    tm = ROW_TILE
    const = lambda i: (0, 0)
    return pl.pallas_call(
        functools.partial(_ffn_kernel, sub=sub, alpha=alpha),
        out_shape=jax.ShapeDtypeStruct((n, D_MODEL), F32),
        grid=(n // tm,),
        in_specs=[
            pl.BlockSpec((tm, D_MODEL), lambda i: (i, 0)),
            pl.BlockSpec((1, N_MOD, D_MODEL), lambda i: (row_map(i), 0, 0)),
            pl.BlockSpec((D_MODEL, 2 * D_FF), const),
            pl.BlockSpec((D_FF, D_MODEL), const),
            pl.BlockSpec((1, D_MODEL), const),
            pl.BlockSpec((1, D_MODEL), const),
        ],
        out_specs=pl.BlockSpec((tm, D_MODEL), lambda i: (i, 0)),
        scratch_shapes=[pltpu.VMEM((tm, D_FF), BF16)],
        compiler_params=_cparams("arbitrary"),
        name="ffn_sublayer",
    )(x, mod, w_in, w_out, ln_g.reshape(1, D_MODEL), ln_b.reshape(1, D_MODEL))


def _inproj_kernel(x_ref, mod_ref, w_ref, *out_refs):
    x = x_ref[...]
    shift = mod_ref[0, 3:4, :]
    scale = mod_ref[0, 4:5, :]
    h = (x * (1.0 + scale) + shift).astype(BF16)
    off = 0
    for o_ref, width in zip(out_refs, IN_SPLITS):
        o_ref[...] = _dot(h, w_ref[:, off:off + width])
        off += width


def _in_projection(x, mod, w_in, *, row_map):
    n = x.shape[0]
    tm = ROW_TILE
    return pl.pallas_call(
        _inproj_kernel,
        out_shape=tuple(jax.ShapeDtypeStruct((n, w), F32) for w in IN_SPLITS),
        grid=(n // tm,),
        in_specs=[
            pl.BlockSpec((tm, D_MODEL), lambda i: (i, 0)),
            pl.BlockSpec((1, N_MOD, D_MODEL), lambda i: (row_map(i), 0, 0)),
            pl.BlockSpec((D_MODEL, IN_WIDTH), lambda i: (0, 0)),
        ],
        out_specs=tuple(pl.BlockSpec((tm, w), lambda i: (i, 0)) for w in IN_SPLITS),
        compiler_params=_cparams("arbitrary"),
        name="in_projection",
    )(x, mod, w_in)


def _rope_tables(T):
    n_rows = T // GRID_W
    row = np.repeat(np.arange(n_rows), GRID_W).astype(np.float32)
    col = np.tile(np.arange(GRID_W), n_rows).astype(np.float32)
    inv = (1.0 / (ROPE_THETA ** (np.arange(0, ROPE_AXIS_DIM, 2, dtype=np.float32) / ROPE_AXIS_DIM))).astype(np.float32)
    ang_r = row[:, None] * inv
    ang_c = col[:, None] * inv
    z = np.zeros_like(ang_r)
    cos = np.concatenate([np.cos(ang_r), np.cos(ang_r), np.cos(ang_c), np.cos(ang_c)], axis=1)
    s_up = np.concatenate([-np.sin(ang_r), z, -np.sin(ang_c), z], axis=1)
    s_dn = np.concatenate([z, np.sin(ang_r), z, np.sin(ang_c)], axis=1)
    tile2 = lambda t: jnp.asarray(np.concatenate([t, t], axis=1), F32)
    return tile2(cos), tile2(s_up), tile2(s_dn)


def _rope128(x, cos, s_up, s_dn):
    up = pltpu.roll(x, LANES - ROPE_AXIS_DIM // 2, axis=1)
    dn = pltpu.roll(x, ROPE_AXIS_DIM // 2, axis=1)
    return x * cos + up * s_up + dn * s_dn


def _softmax_rows(s):
    m = jnp.max(s, axis=-1, keepdims=True)
    e = jnp.exp(s - m)
    return e / jnp.sum(e, axis=-1, keepdims=True)


def _attn_a_kernel(*refs, latent, T, past):
    if latent:
        (aq_ref, ak_ref, av_ref, gq_ref, gk_ref, ones_ref, cos_ref, sup_ref, sdn_ref,
         ck_ref, cv_ref, o_ref, kn_ref, q_s, k_s, v_s) = refs
    else:
        aq_ref, ak_ref, av_ref, gq_ref, gk_ref, ones_ref, o_ref, kn_ref, q_s, k_s, v_s = refs
    ones = ones_ref[...]
    inv_d = 1.0 / HEAD_DIM
    aq = aq_ref[...]
    qn = aq * lax.rsqrt(_seg_sum(aq * aq, ones) * inv_d + RMS_EPS) * gq_ref[...]
    ak = ak_ref[...]
    kn = ak * lax.rsqrt(_seg_sum(ak * ak, ones[:A_KV_W, :A_KV_W]) * inv_d + RMS_EPS) * gk_ref[...]
    kn_ref[...] = kn
    if latent:
        cos, sup, sdn = cos_ref[...], sup_ref[...], sdn_ref[...]
        for j in range(A_Q_W // LANES):
            q_s[:, j * LANES:(j + 1) * LANES] = _rope128(
                qn[:, j * LANES:(j + 1) * LANES], cos, sup, sdn).astype(BF16)
        kr = _rope128(kn, cos, sup, sdn).astype(BF16)
    else:
        q_s[...] = qn.astype(BF16)
        kr = kn.astype(BF16)
    av = av_ref[...].astype(BF16)
    for h in range(ATTN_KV_HEADS):
        k_s[h, 0:T, :] = kr[:, h * HEAD_DIM:(h + 1) * HEAD_DIM]
        v_s[h, 0:T, :] = av[:, h * HEAD_DIM:(h + 1) * HEAD_DIM]
        if latent:
            k_s[h, T:T + past, :] = ck_ref[0, h].astype(BF16)
            v_s[h, T:T + past, :] = cv_ref[0, h].astype(BF16)
    scale = HEAD_DIM ** -0.5
    for qb in range(T // Q_TILE):
        rows = slice(qb * Q_TILE, (qb + 1) * Q_TILE)
        outs = []
        for j in range(ATTN_HEADS):
            h = j // ATTN_GROUP
            q = q_s[rows, j * HEAD_DIM:(j + 1) * HEAD_DIM]
            s = _dot_t(q, k_s[h]) * scale
            p = _softmax_rows(s).astype(BF16)
            outs.append(_dot(p, v_s[h]))
        o_ref[rows, :] = jnp.concatenate(outs, axis=-1)


def _attn_a(aq, ak, av, gq, gk, ones_bd, *, B, T, row0, latent, rope=None, ctx_k=None, ctx_v=None):
    blk0 = row0 // T
    past = ctx_k.shape[2] if latent else 0
    rows = lambda w: pl.BlockSpec((T, w), lambda b: (blk0 + b, 0))
    const = lambda shp: pl.BlockSpec(shp, lambda b: (0,) * len(shp))
    in_specs = [rows(A_Q_W), rows(A_KV_W), rows(A_KV_W), const((1, A_Q_W)), const((1, A_KV_W)),
                const((A_Q_W, A_Q_W))]
    args = [aq, ak, av, gq, gk, ones_bd]
    if latent:
        in_specs += [const((T, LANES))] * 3
        in_specs += [pl.BlockSpec((1, ATTN_KV_HEADS, past, HEAD_DIM), lambda b: (b, 0, 0, 0))] * 2
        args += list(rope) + [ctx_k, ctx_v]
    return pl.pallas_call(
        functools.partial(_attn_a_kernel, latent=latent, T=T, past=past),
        out_shape=(jax.ShapeDtypeStruct((B * T, A_Q_W), F32), jax.ShapeDtypeStruct((B * T, A_KV_W), F32)),
        grid=(B,),
        in_specs=in_specs,
        out_specs=(pl.BlockSpec((T, A_Q_W), lambda b: (b, 0)), pl.BlockSpec((T, A_KV_W), lambda b: (b, 0))),
        scratch_shapes=[pltpu.VMEM((T, A_Q_W), BF16),
                        pltpu.VMEM((ATTN_KV_HEADS, T + past, HEAD_DIM), BF16),
                        pltpu.VMEM((ATTN_KV_HEADS, T + past, HEAD_DIM), BF16)],
        compiler_params=_cparams("arbitrary"),
        name="attn_gqa_latent" if latent else "attn_gqa_context",
    )(*args)


def _attn_b_kernel(*refs, latent, T, past, lam_init):
    if latent:
        (bq_ref, bk_ref, bv_ref, lv_ref, sg_ref, cos_ref, sup_ref, sdn_ref, ck_ref, cv_ref,
         o_ref, k_s, v_s) = refs
    else:
        bq_ref, bk_ref, bv_ref, lv_ref, sg_ref, o_ref, k_s, v_s = refs
    lv = lv_ref[...]
    d01 = jnp.sum(lv[0:1, :] * lv[1:2, :], axis=-1, keepdims=True)
    d23 = jnp.sum(lv[2:3, :] * lv[3:4, :], axis=-1, keepdims=True)
    lam = jnp.exp(d01) - jnp.exp(d23) + lam_init
    q = bq_ref[...]
    k = bk_ref[...]
    if latent:
        cos, sup, sdn = cos_ref[...], sup_ref[...], sdn_ref[...]
        q = _rope128(q, cos, sup, sdn)
        k = _rope128(k, cos, sup, sdn)
    q = q.astype(BF16)
    k_s[0:T, :] = k.astype(BF16)
    v_s[0:T, :] = bv_ref[...].astype(BF16)
    if latent:
        k_s[T:T + past, :] = ck_ref[0, 0].astype(BF16)
        v_s[T:T + past, :] = cv_ref[0, 0].astype(BF16)
    scale = HEAD_DIM ** -0.5
    inv_d = 1.0 / (2 * HEAD_DIM)
    k1 = k_s[:, 0:HEAD_DIM]
    k2 = k_s[:, HEAD_DIM:2 * HEAD_DIM]
    v = v_s[...]
    for qb in range(T // Q_TILE):
        rows = slice(qb * Q_TILE, (qb + 1) * Q_TILE)
        s1 = _dot_t(q[rows, 0:HEAD_DIM], k1) * scale
        s2 = _dot_t(q[rows, HEAD_DIM:2 * HEAD_DIM], k2) * scale
        p = (_softmax_rows(s1) - lam * _softmax_rows(s2)).astype(BF16)
        o = _dot(p, v)
        ms = jnp.mean(o * o, axis=-1, keepdims=True)
        o_ref[rows, :] = o * lax.rsqrt(ms + RMS_EPS) * sg_ref[...] * (1.0 - lam_init)


def _attn_b(bq, bk, bv, lam_vec, subln_g, *, B, T, row0, latent, lam_init, rope=None, ctx_k=None, ctx_v=None):
    blk0 = row0 // T
    past = ctx_k.shape[2] if latent else 0
    dh = 2 * HEAD_DIM
    rows = pl.BlockSpec((T, dh), lambda b, h: (blk0 + b, h))
    const = lambda shp: pl.BlockSpec(shp, lambda b, h: (0,) * len(shp))
    in_specs = [rows, rows, rows, const((4, HEAD_DIM)), const((1, dh))]
    args = [bq, bk, bv, lam_vec, subln_g]
    if latent:
        in_specs += [const((T, LANES))] * 3
        in_specs += [pl.BlockSpec((1, 1, past, dh), lambda b, h: (b, h, 0, 0))] * 2
        args += list(rope) + [ctx_k, ctx_v]
    return pl.pallas_call(
        functools.partial(_attn_b_kernel, latent=latent, T=T, past=past, lam_init=lam_init),
        out_shape=jax.ShapeDtypeStruct((B * T, B_V_W), F32),
        grid=(B, DIFF_HEADS),
        in_specs=in_specs,
        out_specs=pl.BlockSpec((T, dh), lambda b, h: (b, h)),
        scratch_shapes=[pltpu.VMEM((T + past, dh), BF16), pltpu.VMEM((T + past, dh), BF16)],
        compiler_params=_cparams("arbitrary", "arbitrary"),
        name="attn_diff_latent" if latent else "attn_diff_context",
    )(*args)


def _rwkv_prep_kernel(cin_ref, prev_ref, next_ref, mu_ref, w0_ref, w2_ref, a0_ref, a2_ref, g2_ref, kk_ref,
                      ka_ref, rk_ref, ones_ref, r_o, kkn_o, v_o, w_o, a_o, kc_o, g_o, bonus_o,
                      *, tm, n_prompt_blocks, prompt_per_seq, sample_per_seq):
    i = pl.program_id(0)
    in_prompt = i < n_prompt_blocks
    pos = jnp.where(in_prompt, i % prompt_per_seq, (i - n_prompt_blocks) % sample_per_seq)
    per = jnp.where(in_prompt, prompt_per_seq, sample_per_seq)
    row_before = jnp.where(pos == 0, 0.0, prev_ref[SUBLANES - 1:SUBLANES, :])
    row_after = jnp.where(pos == per - 1, 0.0, next_ref[0:1, :])
    x = cin_ref[...]
    t_idx = lax.broadcasted_iota(jnp.int32, (tm, 1), 0)
    prev = jnp.where(t_idx == 0, row_before, pltpu.roll(x, 1, axis=0))
    nxt = jnp.where(t_idx == tm - 1, row_after, pltpu.roll(x, tm - 1, axis=0))
    x = x + mu_ref[...] * (0.5 * (prev + nxt) - x)
    bw = BRANCH_WIDTH
    r = x[:, 0:bw]
    k = x[:, bw:2 * bw]
    v = x[:, 2 * bw:3 * bw]
    off = 3 * bw
    wl = x[:, off:off + 2 * W_LORA]
    al = x[:, off + 2 * W_LORA:off + 2 * W_LORA + 2 * A_LORA]
    gl = x[:, off + 2 * W_LORA + 2 * A_LORA:]
    ones = ones_ref[...]
    kkv = k * kk_ref[...]
    kkn = kkv * lax.rsqrt(_seg_sum(kkv * kkv, ones) + 1e-12)
    g = _dot(_sigmoid(gl).astype(BF16), g2_ref[...])
    r_o[...] = r
    kkn_o[...] = kkn
    v_o[...] = v
    g_o[...] = g
    tw = jnp.tanh(wl).astype(BF16)
    alb = al.astype(BF16)
    decay_rate = math.exp(-0.5)
    bonus = jnp.zeros_like(r)
    for d in range(2):
        w_logit = w0_ref[d:d + 1, :] + _dot(tw[:, d * W_LORA:(d + 1) * W_LORA], w2_ref[d])
        w_o[d] = jnp.exp(-decay_rate * _sigmoid(w_logit))
        a = _sigmoid(a0_ref[d:d + 1, :] + _dot(alb[:, d * A_LORA:(d + 1) * A_LORA], a2_ref[d]))
        a_o[d] = a
        kc = k * (1.0 + (a - 1.0) * ka_ref[...])
        kc_o[d] = kc
        bonus = bonus + _seg_sum(r * kc * rk_ref[...], ones) * v
    bonus_o[...] = bonus


def _rwkv_prep(cin, p, ones_bd, *, n_prompt, t_prompt, t_sample):
    n = cin.shape[0]
    tm = ROW_TILE
    bw = BRANCH_WIDTH
    halo = tm // SUBLANES
    last = n // SUBLANES - 1
    const = lambda shp: pl.BlockSpec(shp, lambda i: (0,) * len(shp))
    tok = jax.ShapeDtypeStruct((n, bw), F32)
    tok2 = jax.ShapeDtypeStruct((2, n, bw), F32)
    tspec = pl.BlockSpec((tm, bw), lambda i: (i, 0))
    tspec2 = pl.BlockSpec((2, tm, bw), lambda i: (0, i, 0))
    return pl.pallas_call(
        functools.partial(_rwkv_prep_kernel, tm=tm, n_prompt_blocks=n_prompt // tm,
                          prompt_per_seq=t_prompt // tm, sample_per_seq=t_sample // tm),
        out_shape=(tok, tok, tok, tok2, tok2, tok2, tok, tok),
        grid=(n // tm,),
        in_specs=[pl.BlockSpec((tm, C_IN_W), lambda i: (i, 0)),
                  pl.BlockSpec((SUBLANES, C_IN_W), lambda i: (jnp.maximum(i * halo - 1, 0), 0)),
                  pl.BlockSpec((SUBLANES, C_IN_W), lambda i: (jnp.minimum((i + 1) * halo, last), 0)),
                  const((1, C_IN_W)), const((2, bw)), const((2, W_LORA, bw)), const((2, bw)),
                  const((2, A_LORA, bw)), const((G_LORA, bw)), const((1, bw)), const((1, bw)), const((1, bw)),
                  const((bw, bw))],
        out_specs=(tspec, tspec, tspec, tspec2, tspec2, tspec2, tspec, tspec),
        compiler_params=_cparams("arbitrary"),
        name="rwkv_streams",
    )(cin, cin, cin, p['mu'], p['w0'], p['w2'], p['a0'], p['a2'], p['g2'], p['k_k'], p['k_a'], p['r_k'], ones_bd)


def _scan_kernel(r_ref, kk_ref, v_ref, w_ref, a_ref, kc_ref, s0_ref, o_ref, sT_ref,
                 S, al_s, wr_s, be_s, *, tc):
    j = pl.program_id(1)
    n = RWKV_HEAD_DIM

    @pl.when(j == 0)
    def _():
        S[...] = s0_ref[0]

    def step(t, carry):
        r = r_ref[0, t]
        kk = kk_ref[0, t]
        v = v_ref[0, t]
        w = w_ref[0, t]
        kc = kc_ref[0, t]
        beta = kk * a_ref[0, t]
        al_s[...] = -kk
        wr_s[...] = w * r
        be_s[...] = beta
        c1 = jnp.sum(beta * r, axis=0, keepdims=True)
        c2 = jnp.sum(kc * r, axis=0, keepdims=True)

        def body_a(k, acc):
            sa, y = acc
            sk = S[k]
            return sa + sk * al_s[pl.ds(k, 1), :], y + sk * wr_s[pl.ds(k, 1), :]

        zero = jnp.zeros((n, LANES), F32)
        sa, y = lax.fori_loop(0, n, body_a, (zero, zero), unroll=8)

        def body_b(k, c):
            S[k] = (S[k] * w_ref[0, t, pl.ds(k, 1), :] + be_s[pl.ds(k, 1), :] * sa
                    + kc_ref[0, t, pl.ds(k, 1), :] * v)
            return c

        lax.fori_loop(0, n, body_b, 0, unroll=8)
        o_ref[0, t] = y + sa * c1 + v * c2
        return carry

    lax.fori_loop(0, tc, step, 0)

    @pl.when(j == pl.num_programs(1) - 1)
    def _():
        sT_ref[0] = S[...]


def _rwkv_scan(streams, s0):
    G, T = streams[0].shape[:2]
    n = RWKV_HEAD_DIM
    tc = SCAN_TC
    sspec = pl.BlockSpec((1, tc, n, LANES), lambda g, j: (g, j, 0, 0))
    stspec = pl.BlockSpec((1, n, n, LANES), lambda g, j: (g, 0, 0, 0))
    return pl.pallas_call(
        functools.partial(_scan_kernel, tc=tc),
        out_shape=(jax.ShapeDtypeStruct((G, T, n, LANES), F32), jax.ShapeDtypeStruct((G, n, n, LANES), F32)),
        grid=(G, T // tc),
        in_specs=[sspec] * 6 + [stspec],
        out_specs=(sspec, stspec),
        scratch_shapes=[pltpu.VMEM((n, n, LANES), F32)] + [pltpu.VMEM((n, LANES), F32)] * 3,
        compiler_params=_cparams("arbitrary", "arbitrary"),
        name="rwkv_scan",
    )(*streams, s0)


def _to_lanes(x, B, T):
    lead = x.shape[:-2]
    x = x.reshape(lead + (B, T, RWKV_HEADS, RWKV_HEAD_DIM))
    nl = len(lead)
    perm = tuple(range(nl)) + (nl + 1, nl + 3, nl + 0, nl + 2)
    return x.transpose(perm).reshape(lead + (T, RWKV_HEAD_DIM, B * RWKV_HEADS))


def _scan_layout(shared, per_dir, B, T):
    nl = 2 * B * RWKV_HEADS
    G = -(-nl // LANES)
    out = []
    for x in shared:
        f = _to_lanes(x, B, T)
        out.append(jnp.concatenate([f, f[::-1]], axis=-1))
    for x in per_dir:
        f = _to_lanes(x, B, T)
        out.append(jnp.concatenate([f[0], f[1][::-1]], axis=-1))
    res = []
    for x in out:
        x = jnp.pad(x, ((0, 0), (0, 0), (0, G * LANES - nl)))
        res.append(x.reshape(T, RWKV_HEAD_DIM, G, LANES).transpose(2, 0, 1, 3))
    return res, G


def _from_scan_layout(o, B, T):
    G = o.shape[0]
    nb = B * RWKV_HEADS
    o = o.transpose(1, 2, 0, 3).reshape(T, RWKV_HEAD_DIM, G * LANES)[:, :, :2 * nb]
    o = o.reshape(T, RWKV_HEAD_DIM, 2, B, RWKV_HEADS)
    tok = lambda z: z.transpose(2, 0, 3, 1).reshape(B * T, BRANCH_WIDTH)
    return tok(o[:, :, 0]), tok(o[::-1, :, 1])


def _state_to_lanes(state0, G):
    B = state0.shape[0]
    nl = 2 * B * RWKV_HEADS
    n = RWKV_HEAD_DIM
    s = state0.transpose(4, 3, 1, 0, 2).reshape(n, n, nl)
    s = jnp.pad(s, ((0, 0), (0, 0), (0, G * LANES - nl)))
    return s.reshape(n, n, G, LANES).transpose(2, 0, 1, 3)


def _state_from_lanes(sT, B):
    G = sT.shape[0]
    n = RWKV_HEAD_DIM
    nl = 2 * B * RWKV_HEADS
    s = sT.transpose(1, 2, 0, 3).reshape(n, n, G * LANES)[:, :, :nl]
    return s.reshape(n, n, 2, B, RWKV_HEADS).transpose(3, 2, 4, 1, 0)


def _merge_kernel(x_ref, mod_ref, gates_ref, oa_ref, ob_ref, of_ref, obw_ref, g_ref, bonus_ref,
                  gng_ref, gnb_ref, ones_ref, wup_ref, wout_ref, lng_ref, lnb_ref, o_ref, *, alpha):
    ones = ones_ref[...]
    inv_n = 1.0 / RWKV_HEAD_DIM
    o = of_ref[...] + obw_ref[...]
    mu = _seg_sum(o, ones) * inv_n
    d = o - mu
    var = _seg_sum(d * d, ones) * inv_n
    oc = d * lax.rsqrt(var + RWKV_GN_EPS) * gng_ref[...] + gnb_ref[...]
    oc = (oc + bonus_ref[...]) * g_ref[...]
    branches = (oa_ref[...], ob_ref[...], oc)
    mix = None
    for i in range(N_BRANCH):
        up = _dot(branches[i].astype(BF16), wup_ref[i])
        term = _sigmoid(gates_ref[:, i * D_MODEL:(i + 1) * D_MODEL]) * up
        mix = term if mix is None else mix + term
    h = _dot(mix.astype(BF16), wout_ref[...])
    x = x_ref[...]
    y = alpha * x + mod_ref[0, 5:6, :] * h
    o_ref[...] = _layer_norm(y, lng_ref[...], lnb_ref[...])


def _merge_sublayer(x, mod, gates, oa, ob, o_f, o_b, g, bonus, gn_g, gn_b, ones_bd, w_up, w_out, ln_g, ln_b,
                    *, alpha, row_map):
    n = x.shape[0]
    tm = ROW_TILE
    bw = BRANCH_WIDTH
    rows = lambda w: pl.BlockSpec((tm, w), lambda i: (i, 0))
    const = lambda shp: pl.BlockSpec(shp, lambda i: (0,) * len(shp))
    return pl.pallas_call(
        functools.partial(_merge_kernel, alpha=alpha),
        out_shape=jax.ShapeDtypeStruct((n, D_MODEL), F32),
        grid=(n // tm,),
        in_specs=[rows(D_MODEL), pl.BlockSpec((1, N_MOD, D_MODEL), lambda i: (row_map(i), 0, 0)),
                  rows(N_BRANCH * D_MODEL), rows(bw), rows(bw), rows(bw), rows(bw), rows(bw), rows(bw),
                  const((1, bw)), const((1, bw)), const((bw, bw)),
                  const((N_BRANCH, bw, D_MODEL)), const((D_MODEL, D_MODEL)),
                  const((1, D_MODEL)), const((1, D_MODEL))],
        out_specs=rows(D_MODEL),
        compiler_params=_cparams("arbitrary"),
        name="merge_sublayer",
    )(x, mod, gates, oa, ob, o_f, o_b, g, bonus, gn_g, gn_b, ones_bd, w_up, w_out,
      ln_g.reshape(1, D_MODEL), ln_b.reshape(1, D_MODEL))


def _heads_first(x, B, T, H):
    return x.reshape(B, T, H, x.shape[-1] // H).transpose(0, 2, 1, 3)


def kernel(x_prompt, x_sample, cache_attn_k, cache_attn_v, cache_diff_k, cache_diff_v, state_rwkv, c, c_ctx,
           w_mod, b_mod, ln_g, ln_b, ffn_w_in, ffn_w_out, w_in, qk_norm_g, diff_lambda, diff_subln_g,
           rwkv_mu, rwkv_w0, rwkv_w2, rwkv_a0, rwkv_a2, rwkv_g2, rwkv_k_k, rwkv_k_a, rwkv_r_k,
           rwkv_gn_g, rwkv_gn_b, w_up, w_out):
    Bp, Tp, _ = x_prompt.shape
    Bs, Ts, _ = x_sample.shape
    depth = w_mod.shape[0]
    alpha = (2.0 * depth) ** 0.25
    n_p, n_s = Bp * Tp, Bs * Ts
    assert n_p % ROW_TILE == 0 and Ts % ROW_TILE == 0 and 1 + Bs <= MOD_ROWS
    row_map = _mod_row_map(n_p // ROW_TILE, Ts // ROW_TILE)

    x = jnp.concatenate([x_prompt.reshape(n_p, D_MODEL), x_sample.reshape(n_s, D_MODEL)], axis=0)
    cvecs = jnp.concatenate([c_ctx[None], c, jnp.zeros((MOD_ROWS - 1 - Bs, D_MODEL), F32)], axis=0)
    mod_all = _modulation(cvecs, w_mod, b_mod)

    seg = np.arange(BRANCH_WIDTH) // HEAD_DIM
    ones_bd = jnp.asarray(seg[:, None] == seg[None, :], BF16)
    rope = _rope_tables(Ts)
    bw = BRANCH_WIDTH

    n_ak, n_av, n_dk, n_dv, n_st = [], [], [], [], []
    for l in range(depth):
        lam_init = 0.8 - 0.6 * math.exp(-0.3 * l)
        mod = mod_all[l]
        x = _ffn_sublayer(x, mod, ffn_w_in[l, 0].astype(BF16), ffn_w_out[l, 0].astype(BF16),
                          ln_g[l, 0], ln_b[l, 0], sub=0, alpha=alpha, row_map=row_map)
        gates, aq, ak, av, bq, bk, bv, cin = _in_projection(x, mod, w_in[l].astype(BF16), row_map=row_map)

        gq = jnp.tile(qk_norm_g[l, 0], ATTN_HEADS)[None]
        gk = jnp.tile(qk_norm_g[l, 1], ATTN_KV_HEADS)[None]
        oa_p, kn_p = _attn_a(aq, ak, av, gq, gk, ones_bd, B=Bp, T=Tp, row0=0, latent=False)
        oa_s, _ = _attn_a(aq, ak, av, gq, gk, ones_bd, B=Bs, T=Ts, row0=n_p, latent=True, rope=rope,
                          ctx_k=cache_attn_k[:, l], ctx_v=cache_attn_v[:, l])
        sg = diff_subln_g[l][None]
        ob_p = _attn_b(bq, bk, bv, diff_lambda[l], sg, B=Bp, T=Tp, row0=0, latent=False, lam_init=lam_init)
        ob_s = _attn_b(bq, bk, bv, diff_lambda[l], sg, B=Bs, T=Ts, row0=n_p, latent=True, lam_init=lam_init,
                       rope=rope, ctx_k=cache_diff_k[:, l], ctx_v=cache_diff_v[:, l])

        rp = {'mu': rwkv_mu[l][None], 'w0': rwkv_w0[l], 'w2': rwkv_w2[l].astype(BF16), 'a0': rwkv_a0[l],
              'a2': rwkv_a2[l].astype(BF16), 'g2': rwkv_g2[l].astype(BF16), 'k_k': rwkv_k_k[l][None],
              'k_a': rwkv_k_a[l][None], 'r_k': rwkv_r_k[l].reshape(1, bw)}
        r, kkn, v, w, a, kc, gg, bonus = _rwkv_prep(cin, rp, ones_bd, n_prompt=n_p, t_prompt=Tp, t_sample=Ts)
        of, ob = [], []
        for (B, T, row0, st0) in ((Bp, Tp, 0, None), (Bs, Ts, n_p, state_rwkv[:, l])):
            rows = slice(row0, row0 + B * T)
            streams, G = _scan_layout((r[rows], kkn[rows], v[rows]), (w[:, rows], a[:, rows], kc[:, rows]), B, T)
            if st0 is None:
                s0 = jnp.zeros((G, RWKV_HEAD_DIM, RWKV_HEAD_DIM, LANES), F32)
            else:
                s0 = _state_to_lanes(st0, G)
            o, sT = _rwkv_scan(streams, s0)
            o_f, o_b = _from_scan_layout(o, B, T)
            of.append(o_f)
            ob.append(o_b)
            if st0 is None:
                n_st.append(_state_from_lanes(sT, B))
        cat = lambda parts: jnp.concatenate(parts, axis=0)
        x = _merge_sublayer(x, mod, gates, cat([oa_p, oa_s]), cat([ob_p, ob_s]), cat(of), cat(ob), gg,
                            bonus, rwkv_gn_g[l][None], rwkv_gn_b[l][None], ones_bd,
                            w_up[l].astype(BF16), w_out[l].astype(BF16), ln_g[l, 1], ln_b[l, 1],
                            alpha=alpha, row_map=row_map)
        x = _ffn_sublayer(x, mod, ffn_w_in[l, 1].astype(BF16), ffn_w_out[l, 1].astype(BF16),
                          ln_g[l, 2], ln_b[l, 2], sub=2, alpha=alpha, row_map=row_map)

        n_ak.append(_heads_first(kn_p, Bp, Tp, ATTN_KV_HEADS))
        n_av.append(_heads_first(av[:n_p], Bp, Tp, ATTN_KV_HEADS))
        n_dk.append(_heads_first(bk[:n_p], Bp, Tp, DIFF_HEADS))
        n_dv.append(_heads_first(bv[:n_p], Bp, Tp, DIFF_HEADS))

    y_prompt = x[:n_p].reshape(Bp, Tp, D_MODEL)
    y_sample = x[n_p:].reshape(Bs, Ts, D_MODEL)
    stack = lambda parts: jnp.stack(parts, axis=1)
    return (y_prompt, y_sample, stack(n_ak), stack(n_av), stack(n_dk), stack(n_dv), stack(n_st))
```

```python
import functools
import math

import numpy as np
import jax
import jax.numpy as jnp
from jax import lax
from jax.experimental import pallas as pl
from jax.experimental.pallas import tpu as pltpu

F32 = jnp.float32
BF16 = jnp.bfloat16

D_MODEL = 1024
GRID_W = 64
HEAD_DIM = 64
ATTN_HEADS = 8
ATTN_KV_HEADS = 2
ATTN_GROUP = ATTN_HEADS // ATTN_KV_HEADS
DIFF_HEADS = 4
RWKV_HEADS = 8
RWKV_HEAD_DIM = 64
BRANCH_WIDTH = 512
W_LORA = 64
A_LORA = 64
G_LORA = 128
D_FF = 2816
N_BRANCH = 3
N_MOD = 9
ROPE_THETA = 10000.0
ROPE_AXIS_DIM = HEAD_DIM // 2
LN_EPS = 1e-5
RMS_EPS = 1e-6
RWKV_GN_EPS = 64e-5
A_Q_W = ATTN_HEADS * HEAD_DIM
A_KV_W = ATTN_KV_HEADS * HEAD_DIM
B_QK_W = DIFF_HEADS * 2 * HEAD_DIM
B_V_W = DIFF_HEADS * 2 * HEAD_DIM
C_IN_W = 3 * BRANCH_WIDTH + 2 * W_LORA + 2 * A_LORA + G_LORA
IN_SPLITS = (N_BRANCH * D_MODEL, A_Q_W, A_KV_W, A_KV_W, B_QK_W, B_QK_W, B_V_W, C_IN_W)
IN_WIDTH = sum(IN_SPLITS)

LANES = 128
SUBLANES = 8
VMEM_LIMIT_BYTES = 56 * 1024 * 1024

MOD_ROWS = 8
MOD_TN = 1152
FFN_CK = 256
ROW_TILE = 256
Q_TILE = 256
SCAN_TC = 32


def _cparams(*sem):
    return pltpu.CompilerParams(dimension_semantics=sem, vmem_limit_bytes=VMEM_LIMIT_BYTES)


def _sigmoid(x):
    return 1.0 / (1.0 + jnp.exp(-x))


def _dot(a, b):
    return jnp.dot(a, b, preferred_element_type=F32)


def _dot_t(a, b):
    return lax.dot_general(a, b, (((1,), (1,)), ((), ())), preferred_element_type=F32)


def _seg_sum(x, ones_bd):
    hi = x.astype(BF16)
    r1 = x - hi.astype(F32)
    mid = r1.astype(BF16)
    lo = (r1 - mid.astype(F32)).astype(BF16)
    return _dot(hi, ones_bd) + _dot(mid, ones_bd) + _dot(lo, ones_bd)


def _layer_norm(y, g, b):
    mu = jnp.mean(y, axis=-1, keepdims=True)
    d = y - mu
    var = jnp.mean(d * d, axis=-1, keepdims=True)
    return d * lax.rsqrt(var + LN_EPS) * g + b


def _mod_row_map(n_prompt_blocks, blocks_per_sample):
    def row(i):
        return jnp.where(i < n_prompt_blocks, 0, 1 + (i - n_prompt_blocks) // blocks_per_sample)
    return row


def _mod_kernel(c_ref, w_ref, b_ref, o_ref):
    cv = c_ref[...]
    h = (cv * _sigmoid(cv)).astype(BF16)
    o_ref[0] = _dot(h, w_ref[0].astype(BF16)) + b_ref[0]


def _modulation(cvecs, w_mod, b_mod):
    L = w_mod.shape[0]
    n = N_MOD * D_MODEL
    out = pl.pallas_call(
        _mod_kernel,
        out_shape=jax.ShapeDtypeStruct((L, MOD_ROWS, n), F32),
        grid=(L, n // MOD_TN),
        in_specs=[
            pl.BlockSpec((MOD_ROWS, D_MODEL), lambda l, j: (0, 0)),
            pl.BlockSpec((1, D_MODEL, MOD_TN), lambda l, j: (l, 0, j)),
            pl.BlockSpec((1, 1, MOD_TN), lambda l, j: (l, 0, j)),
        ],
        out_specs=pl.BlockSpec((1, MOD_ROWS, MOD_TN), lambda l, j: (l, 0, j)),
        compiler_params=_cparams("arbitrary", "arbitrary"),
        name="modulation",
    )(cvecs, w_mod, b_mod.reshape(L, 1, n))
    return out.reshape(L, MOD_ROWS, N_MOD, D_MODEL)


def _ffn_kernel(x_ref, mod_ref, win_ref, wout_ref, g_ref, b_ref, o_ref, act_ref, *, sub, alpha):
    x = x_ref[...]
    shift = mod_ref[0, 3 * sub:3 * sub + 1, :]
    scale = mod_ref[0, 3 * sub + 1:3 * sub + 2, :]
    gate = mod_ref[0, 3 * sub + 2:3 * sub + 3, :]
    h = (x * (1.0 + scale) + shift).astype(BF16)
    for c in range(D_FF // FFN_CK):
        a = _dot(h, win_ref[:, c * FFN_CK:(c + 1) * FFN_CK])
        b = _dot(h, win_ref[:, D_FF + c * FFN_CK:D_FF + (c + 1) * FFN_CK])
        act_ref[:, c * FFN_CK:(c + 1) * FFN_CK] = (a * _sigmoid(a) * b).astype(BF16)
    f = _dot(act_ref[...], wout_ref[...])
    y = alpha * x + (0.5 * gate) * f
    o_ref[...] = _layer_norm(y, g_ref[...], b_ref[...])


def _ffn_sublayer(x, mod, w_in, w_out, ln_g, ln_b, *, sub, alpha, row_map):
    n = x.shape[0]
    tm = ROW_TILE
    const = lambda i: (0, 0)
    return pl.pallas_call(
        functools.partial(_ffn_kernel, sub=sub, alpha=alpha),
        out_shape=jax.ShapeDtypeStruct((n, D_MODEL), F32),
        grid=(n // tm,),
        in_specs=[
            pl.BlockSpec((tm, D_MODEL), lambda i: (i, 0)),
            pl.BlockSpec((1, N_MOD, D_MODEL), lambda i: (row_map(i), 0, 0)),
            pl.BlockSpec((D_MODEL, 2 * D_FF), const),
            pl.BlockSpec((D_FF, D_MODEL), const),
            pl.BlockSpec((1, D_MODEL), const),
            pl.BlockSpec((1, D_MODEL), const),
        ],
        out_specs=pl.BlockSpec((tm, D_MODEL), lambda i: (i, 0)),
        scratch_shapes=[pltpu.VMEM((tm, D_FF), BF16)],
        compiler_params=_cparams("arbitrary"),
        name="ffn_sublayer",
    )(x, mod, w_in, w_out, ln_g.reshape(1, D_MODEL), ln_b.reshape(1, D_MODEL))


def _inproj_kernel(x_ref, mod_ref, w_ref, *out_refs):
    x = x_ref[...]
    shift = mod_ref[0, 3:4, :]
    scale = mod_ref[0, 4:5, :]
    h = (x * (1.0 + scale) + shift).astype(BF16)
    off = 0
    for o_ref, width in zip(out_refs, IN_SPLITS):
        o_ref[...] = _dot(h, w_ref[:, off:off + width])
        off += width


def _in_projection(x, mod, w_in, *, row_map):
    n = x.shape[0]
    tm = ROW_TILE
    return pl.pallas_call(
        _inproj_kernel,
        out_shape=tuple(jax.ShapeDtypeStruct((n, w), F32) for w in IN_SPLITS),
        grid=(n // tm,),
        in_specs=[
            pl.BlockSpec((tm, D_MODEL), lambda i: (i, 0)),
            pl.BlockSpec((1, N_MOD, D_MODEL), lambda i: (row_map(i), 0, 0)),
            pl.BlockSpec((D_MODEL, IN_WIDTH), lambda i: (0, 0)),
        ],
        out_specs=tuple(pl.BlockSpec((tm, w), lambda i: (i, 0)) for w in IN_SPLITS),
        compiler_params=_cparams("arbitrary"),
        name="in_projection",
    )(x, mod, w_in)


def _rope_tables(T):
    n_rows = T // GRID_W
    row = np.repeat(np.arange(n_rows), GRID_W).astype(np.float32)
    col = np.tile(np.arange(GRID_W), n_rows).astype(np.float32)
    inv = (1.0 / (ROPE_THETA ** (np.arange(0, ROPE_AXIS_DIM, 2, dtype=np.float32) / ROPE_AXIS_DIM))).astype(np.float32)
    ang_r = row[:, None] * inv
    ang_c = col[:, None] * inv
    z = np.zeros_like(ang_r)
    cos = np.concatenate([np.cos(ang_r), np.cos(ang_r), np.cos(ang_c), np.cos(ang_c)], axis=1)
    s_up = np.concatenate([-np.sin(ang_r), z, -np.sin(ang_c), z], axis=1)
    s_dn = np.concatenate([z, np.sin(ang_r), z, np.sin(ang_c)], axis=1)
    tile2 = lambda t: jnp.asarray(np.concatenate([t, t], axis=1), F32)
    return tile2(cos), tile2(s_up), tile2(s_dn)


def _rope128(x, cos, s_up, s_dn):
    up = pltpu.roll(x, LANES - ROPE_AXIS_DIM // 2, axis=1)
    dn = pltpu.roll(x, ROPE_AXIS_DIM // 2, axis=1)
    return x * cos + up * s_up + dn * s_dn


def _softmax_rows(s):
    m = jnp.max(s, axis=-1, keepdims=True)
    e = jnp.exp(s - m)
    return e / jnp.sum(e, axis=-1, keepdims=True)


def _attn_a_kernel(*refs, latent, T, past):
    if latent:
        (aq_ref, ak_ref, av_ref, gq_ref, gk_ref, ones_ref, cos_ref, sup_ref, sdn_ref,
         ck_ref, cv_ref, o_ref, kn_ref, q_s, k_s, v_s) = refs
    else:
        aq_ref, ak_ref, av_ref, gq_ref, gk_ref, ones_ref, o_ref, kn_ref, q_s, k_s, v_s = refs
    ones = ones_ref[...]
    inv_d = 1.0 / HEAD_DIM
    aq = aq_ref[...]
    qn = aq * lax.rsqrt(_seg_sum(aq * aq, ones) * inv_d + RMS_EPS) * gq_ref[...]
    ak = ak_ref[...]
    kn = ak * lax.rsqrt(_seg_sum(ak * ak, ones[:A_KV_W, :A_KV_W]) * inv_d + RMS_EPS) * gk_ref[...]
    kn_ref[...] = kn
    if latent:
        cos, sup, sdn = cos_ref[...], sup_ref[...], sdn_ref[...]
        for j in range(A_Q_W // LANES):
            q_s[:, j * LANES:(j + 1) * LANES] = _rope128(
                qn[:, j * LANES:(j + 1) * LANES], cos, sup, sdn).astype(BF16)
        kr = _rope128(kn, cos, sup, sdn).astype(BF16)
    else:
        q_s[...] = qn.astype(BF16)
        kr = kn.astype(BF16)
    av = av_ref[...].astype(BF16)
    for h in range(ATTN_KV_HEADS):
        k_s[h, 0:T, :] = kr[:, h * HEAD_DIM:(h + 1) * HEAD_DIM]
        v_s[h, 0:T, :] = av[:, h * HEAD_DIM:(h + 1) * HEAD_DIM]
        if latent:
            k_s[h, T:T + past, :] = ck_ref[0, h].astype(BF16)
            v_s[h, T:T + past, :] = cv_ref[0, h].astype(BF16)
    scale = HEAD_DIM ** -0.5
    for qb in range(T // Q_TILE):
        rows = slice(qb * Q_TILE, (qb + 1) * Q_TILE)
        outs = []
        for j in range(ATTN_HEADS):
            h = j // ATTN_GROUP
            q = q_s[rows, j * HEAD_DIM:(j + 1) * HEAD_DIM]
            s = _dot_t(q, k_s[h]) * scale
            p = _softmax_rows(s).astype(BF16)
            outs.append(_dot(p, v_s[h]))
        o_ref[rows, :] = jnp.concatenate(outs, axis=-1)


def _attn_a(aq, ak, av, gq, gk, ones_bd, *, B, T, row0, latent, rope=None, ctx_k=None, ctx_v=None):
    blk0 = row0 // T
    past = ctx_k.shape[2] if latent else 0
    rows = lambda w: pl.BlockSpec((T, w), lambda b: (blk0 + b, 0))
    const = lambda shp: pl.BlockSpec(shp, lambda b: (0,) * len(shp))
    in_specs = [rows(A_Q_W), rows(A_KV_W), rows(A_KV_W), const((1, A_Q_W)), const((1, A_KV_W)),
                const((A_Q_W, A_Q_W))]
    args = [aq, ak, av, gq, gk, ones_bd]
    if latent:
        in_specs += [const((T, LANES))] * 3
        in_specs += [pl.BlockSpec((1, ATTN_KV_HEADS, past, HEAD_DIM), lambda b: (b, 0, 0, 0))] * 2
        args += list(rope) + [ctx_k, ctx_v]
    return pl.pallas_call(
        functools.partial(_attn_a_kernel, latent=latent, T=T, past=past),
        out_shape=(jax.ShapeDtypeStruct((B * T, A_Q_W), F32), jax.ShapeDtypeStruct((B * T, A_KV_W), F32)),
        grid=(B,),
        in_specs=in_specs,
        out_specs=(pl.BlockSpec((T, A_Q_W), lambda b: (b, 0)), pl.BlockSpec((T, A_KV_W), lambda b: (b, 0))),
        scratch_shapes=[pltpu.VMEM((T, A_Q_W), BF16),
                        pltpu.VMEM((ATTN_KV_HEADS, T + past, HEAD_DIM), BF16),
                        pltpu.VMEM((ATTN_KV_HEADS, T + past, HEAD_DIM), BF16)],
        compiler_params=_cparams("arbitrary"),
        name="attn_gqa_latent" if latent else "attn_gqa_context",
    )(*args)


def _attn_b_kernel(*refs, latent, T, past, lam_init):
    if latent:
        (bq_ref, bk_ref, bv_ref, lv_ref, sg_ref, cos_ref, sup_ref, sdn_ref, ck_ref, cv_ref,
         o_ref, k_s, v_s) = refs
    else:
        bq_ref, bk_ref, bv_ref, lv_ref, sg_ref, o_ref, k_s, v_s = refs
    lv = lv_ref[...]
    d01 = jnp.sum(lv[0:1, :] * lv[1:2, :], axis=-1, keepdims=True)
    d23 = jnp.sum(lv[2:3, :] * lv[3:4, :], axis=-1, keepdims=True)
    lam = jnp.exp(d01) - jnp.exp(d23) + lam_init
    q = bq_ref[...]
    k = bk_ref[...]
    if latent:
        cos, sup, sdn = cos_ref[...], sup_ref[...], sdn_ref[...]
        q = _rope128(q, cos, sup, sdn)
        k = _rope128(k, cos, sup, sdn)
    q = q.astype(BF16)
    k_s[0:T, :] = k.astype(BF16)
    v_s[0:T, :] = bv_ref[...].astype(BF16)
    if latent:
        k_s[T:T + past, :] = ck_ref[0, 0].astype(BF16)
        v_s[T:T + past, :] = cv_ref[0, 0].astype(BF16)
    scale = HEAD_DIM ** -0.5
    inv_d = 1.0 / (2 * HEAD_DIM)
    k1 = k_s[:, 0:HEAD_DIM]
    k2 = k_s[:, HEAD_DIM:2 * HEAD_DIM]
    v = v_s[...]
    for qb in range(T // Q_TILE):
        rows = slice(qb * Q_TILE, (qb + 1) * Q_TILE)
        s1 = _dot_t(q[rows, 0:HEAD_DIM], k1) * scale
        s2 = _dot_t(q[rows, HEAD_DIM:2 * HEAD_DIM], k2) * scale
        p = (_softmax_rows(s1) - lam * _softmax_rows(s2)).astype(BF16)
        o = _dot(p, v)
        ms = jnp.mean(o * o, axis=-1, keepdims=True)
        o_ref[rows, :] = o * lax.rsqrt(ms + RMS_EPS) * sg_ref[...] * (1.0 - lam_init)


def _attn_b(bq, bk, bv, lam_vec, subln_g, *, B, T, row0, latent, lam_init, rope=None, ctx_k=None, ctx_v=None):
    blk0 = row0 // T
    past = ctx_k.shape[2] if latent else 0
    dh = 2 * HEAD_DIM
    rows = pl.BlockSpec((T, dh), lambda b, h: (blk0 + b, h))
    const = lambda shp: pl.BlockSpec(shp, lambda b, h: (0,) * len(shp))
    in_specs = [rows, rows, rows, const((4, HEAD_DIM)), const((1, dh))]
    args = [bq, bk, bv, lam_vec, subln_g]
    if latent:
        in_specs += [const((T, LANES))] * 3
        in_specs += [pl.BlockSpec((1, 1, past, dh), lambda b, h: (b, h, 0, 0))] * 2
        args += list(rope) + [ctx_k, ctx_v]
    return pl.pallas_call(
        functools.partial(_attn_b_kernel, latent=latent, T=T, past=past, lam_init=lam_init),
        out_shape=jax.ShapeDtypeStruct((B * T, B_V_W), F32),
        grid=(B, DIFF_HEADS),
        in_specs=in_specs,
        out_specs=pl.BlockSpec((T, dh), lambda b, h: (b, h)),
        scratch_shapes=[pltpu.VMEM((T + past, dh), BF16), pltpu.VMEM((T + past, dh), BF16)],
        compiler_params=_cparams("arbitrary", "arbitrary"),
        name="attn_diff_latent" if latent else "attn_diff_context",
    )(*args)


def _rwkv_prep_kernel(cin_ref, prev_ref, next_ref, mu_ref, w0_ref, w2_ref, a0_ref, a2_ref, g2_ref,
                      ka_ref, rk_ref, ones_ref, r_o, k_o, v_o, w_o, a_o, g_o, bonus_o,
                      *, tm, n_prompt_blocks, prompt_per_seq, sample_per_seq):
    i = pl.program_id(0)
    in_prompt = i < n_prompt_blocks
    pos = jnp.where(in_prompt, i % prompt_per_seq, (i - n_prompt_blocks) % sample_per_seq)
    per = jnp.where(in_prompt, prompt_per_seq, sample_per_seq)
    row_before = jnp.where(pos == 0, 0.0, prev_ref[SUBLANES - 1:SUBLANES, :])
    row_after = jnp.where(pos == per - 1, 0.0, next_ref[0:1, :])
    x = cin_ref[...]
    t_idx = lax.broadcasted_iota(jnp.int32, (tm, 1), 0)
    prev = jnp.where(t_idx == 0, row_before, pltpu.roll(x, 1, axis=0))
    nxt = jnp.where(t_idx == tm - 1, row_after, pltpu.roll(x, tm - 1, axis=0))
    x = x + mu_ref[...] * (0.5 * (prev + nxt) - x)
    bw = BRANCH_WIDTH
    r = x[:, 0:bw]
    k = x[:, bw:2 * bw]
    v = x[:, 2 * bw:3 * bw]
    off = 3 * bw
    wl = x[:, off:off + 2 * W_LORA]
    al = x[:, off + 2 * W_LORA:off + 2 * W_LORA + 2 * A_LORA]
    gl = x[:, off + 2 * W_LORA + 2 * A_LORA:]
    ones = ones_ref[...]
    r_o[...] = r
    k_o[...] = k
    v_o[...] = v
    g_o[...] = _dot(_sigmoid(gl).astype(BF16), g2_ref[...])
    tw = jnp.tanh(wl).astype(BF16)
    alb = al.astype(BF16)
    decay_rate = math.exp(-0.5)
    bonus = jnp.zeros_like(r)
    for d in range(2):
        w_logit = w0_ref[d:d + 1, :] + _dot(tw[:, d * W_LORA:(d + 1) * W_LORA], w2_ref[d])
        w_o[d] = jnp.exp(-decay_rate * _sigmoid(w_logit))
        a = _sigmoid(a0_ref[d:d + 1, :] + _dot(alb[:, d * A_LORA:(d + 1) * A_LORA], a2_ref[d]))
        a_o[d] = a
        kc = k * (1.0 + (a - 1.0) * ka_ref[...])
        bonus = bonus + _seg_sum(r * kc * rk_ref[...], ones) * v
    bonus_o[...] = bonus


def _rwkv_prep(cin, p, ones_bd, *, n_prompt, t_prompt, t_sample):
    n = cin.shape[0]
    tm = ROW_TILE
    bw = BRANCH_WIDTH
    halo = tm // SUBLANES
    last = n // SUBLANES - 1
    const = lambda shp: pl.BlockSpec(shp, lambda i: (0,) * len(shp))
    tok = jax.ShapeDtypeStruct((n, bw), F32)
    tok2 = jax.ShapeDtypeStruct((2, n, bw), F32)
    tspec = pl.BlockSpec((tm, bw), lambda i: (i, 0))
    tspec2 = pl.BlockSpec((2, tm, bw), lambda i: (0, i, 0))
    return pl.pallas_call(
        functools.partial(_rwkv_prep_kernel, tm=tm, n_prompt_blocks=n_prompt // tm,
                          prompt_per_seq=t_prompt // tm, sample_per_seq=t_sample // tm),
        out_shape=(tok, tok, tok, tok2, tok2, tok, tok),
        grid=(n // tm,),
        in_specs=[pl.BlockSpec((tm, C_IN_W), lambda i: (i, 0)),
                  pl.BlockSpec((SUBLANES, C_IN_W), lambda i: (jnp.maximum(i * halo - 1, 0), 0)),
                  pl.BlockSpec((SUBLANES, C_IN_W), lambda i: (jnp.minimum((i + 1) * halo, last), 0)),
                  const((1, C_IN_W)), const((2, bw)), const((2, W_LORA, bw)), const((2, bw)),
                  const((2, A_LORA, bw)), const((G_LORA, bw)), const((1, bw)), const((1, bw)),
                  const((bw, bw))],
        out_specs=(tspec, tspec, tspec, tspec2, tspec2, tspec, tspec),
        compiler_params=_cparams("arbitrary"),
        name="rwkv_streams",
    )(cin, cin, cin, p['mu'], p['w0'], p['w2'], p['a0'], p['a2'], p['g2'], p['k_a'], p['r_k'], ones_bd)


def _scan_kernel(*refs, tc, nv, dir_lanes):
    if dir_lanes:
        (rf, kf, vf, wf, af, rb, kb, vb, wb, ab, bwd_ref, kk_ref, ka_ref, s0_ref,
         of_ref, ob_ref, sT_ref, S, al_s, wr_s, be_s, kc_s, w_s) = refs
        bwd = bwd_ref[...] > 0.0
    else:
        (rf, kf, vf, wf, af, kk_ref, ka_ref, s0_ref, of_ref, sT_ref, S, al_s, wr_s, be_s, kc_s, w_s) = refs
    d = pl.program_id(0)
    j = pl.program_id(2)
    n = RWKV_HEAD_DIM

    @pl.when(j == 0)
    def _():
        S[...] = s0_ref[...]

    def step(i, carry):
        if dir_lanes:
            ib = tc - 1 - i
            r = jnp.where(bwd, rb[ib], rf[i])
            k = jnp.where(bwd, kb[ib], kf[i])
            v = jnp.where(bwd, vb[ib], vf[i])
            w = jnp.where(bwd, wb[ib], wf[i])
            a = jnp.where(bwd, ab[ib], af[i])
        else:
            row = i + d * (tc - 1 - 2 * i)
            r, k, v, w, a = rf[row], kf[row], vf[row], wf[row], af[row]
        kkv = k * kk_ref[...]
        kk = kkv * lax.rsqrt(jnp.sum(kkv * kkv, axis=0, keepdims=True) + 1e-12)
        kc = k * (1.0 + (a - 1.0) * ka_ref[...])
        beta = kk * a
        al_s[...] = -kk
        wr_s[...] = w * r
        be_s[...] = beta
        kc_s[...] = kc
        w_s[...] = w
        c1 = jnp.sum(beta * r, axis=0, keepdims=True)
        c2 = jnp.sum(kc * r, axis=0, keepdims=True)

        def body_a(kx, acc):
            sa, y = acc
            sk = S[kx]
            return sa + sk * al_s[pl.ds(kx, 1), :], y + sk * wr_s[pl.ds(kx, 1), :]

        zero = jnp.zeros((nv, LANES), F32)
        sa, y = lax.fori_loop(0, n, body_a, (zero, zero), unroll=8)

        def body_b(kx, c):
            S[kx] = S[kx] * w_s[pl.ds(kx, 1), :] + be_s[pl.ds(kx, 1), :] * sa + kc_s[pl.ds(kx, 1), :] * v
            return c

        lax.fori_loop(0, n, body_b, 0, unroll=8)
        out = y + sa * c1 + v * c2
        if dir_lanes:
            of_ref[i] = out
            ob_ref[ib] = out
        else:
            of_ref[row] = out
        return carry

    lax.fori_loop(0, tc, step, 0)

    @pl.when(j == pl.num_programs(2) - 1)
    def _():
        sT_ref[...] = S[...]


def _scan_scratch(nv):
    n = RWKV_HEAD_DIM
    return [pltpu.VMEM((n, nv, LANES), F32)] + [pltpu.VMEM((n, LANES), F32)] * 5


def _rwkv_scan_groups(r, k, v, w, a, kk_t, ka_t, s0):
    LG, T = r.shape[:2]
    n = RWKV_HEAD_DIM
    tc = SCAN_TC
    nt = T // tc
    tb = lambda d, j: j + d * (nt - 1 - 2 * j)
    shared = pl.BlockSpec((None, tc, n, LANES), lambda d, g, j: (g, tb(d, j), 0, 0))
    perdir = pl.BlockSpec((None, None, tc, n, LANES), lambda d, g, j: (d, g, tb(d, j), 0, 0))
    par = pl.BlockSpec((n, LANES), lambda d, g, j: (0, 0))
    st = pl.BlockSpec((None, None, n, n, LANES), lambda d, g, j: (d, g, 0, 0, 0))
    return pl.pallas_call(
        functools.partial(_scan_kernel, tc=tc, nv=n, dir_lanes=False),
        out_shape=(jax.ShapeDtypeStruct((2, LG, T, n, LANES), F32), jax.ShapeDtypeStruct((2, LG, n, n, LANES), F32)),
        grid=(2, LG, nt),
        in_specs=[shared, shared, shared, perdir, perdir, par, par, st],
        out_specs=(perdir, st),
        scratch_shapes=_scan_scratch(n),
        compiler_params=_cparams("arbitrary", "arbitrary", "arbitrary"),
        name="rwkv_scan_context",
    )(r, k, v, w, a, kk_t, ka_t, s0)


def _rwkv_scan_lanes(r, k, v, w, a, bwd, kk_t, ka_t, s0):
    T = r.shape[0]
    nv = v.shape[1]
    n = RWKV_HEAD_DIM
    tc = SCAN_TC
    nt = T // tc
    fwd_k = pl.BlockSpec((tc, n, LANES), lambda d, g, j: (j, 0, 0))
    bwd_k = pl.BlockSpec((tc, n, LANES), lambda d, g, j: (nt - 1 - j, 0, 0))
    fwd_v = pl.BlockSpec((tc, nv, LANES), lambda d, g, j: (j, 0, 0))
    bwd_v = pl.BlockSpec((tc, nv, LANES), lambda d, g, j: (nt - 1 - j, 0, 0))
    par = pl.BlockSpec((n, LANES), lambda d, g, j: (0, 0))
    st = pl.BlockSpec((n, nv, LANES), lambda d, g, j: (0, 0, 0))
    o_sds = jax.ShapeDtypeStruct((T, nv, LANES), F32)
    return pl.pallas_call(
        functools.partial(_scan_kernel, tc=tc, nv=nv, dir_lanes=True),
        out_shape=(o_sds, o_sds, jax.ShapeDtypeStruct((n, nv, LANES), F32)),
        grid=(1, 1, nt),
        in_specs=[fwd_k, fwd_k, fwd_v, fwd_k, fwd_k, bwd_k, bwd_k, bwd_v, bwd_k, bwd_k,
                  pl.BlockSpec((1, LANES), lambda d, g, j: (0, 0)), par, par, st],
        out_specs=(fwd_v, bwd_v, st),
        scratch_shapes=_scan_scratch(nv),
        compiler_params=_cparams("arbitrary", "arbitrary", "arbitrary"),
        name="rwkv_scan_latent",
    )(r, k, v, w, a, r, k, v, w, a, bwd, kk_t, ka_t, s0)


def _to_lanes(x, B, T):
    lead = x.shape[:-2]
    x = x.reshape(lead + (B, T, RWKV_HEADS, RWKV_HEAD_DIM))
    nl = len(lead)
    perm = tuple(range(nl)) + (nl + 1, nl + 3, nl + 0, nl + 2)
    return x.transpose(perm).reshape(lead + (T, RWKV_HEAD_DIM, B * RWKV_HEADS))


def _lane_param(p, vl):
    t = jnp.repeat(p.reshape(RWKV_HEADS, RWKV_HEAD_DIM).T, vl, axis=1)
    return jnp.tile(t, (1, LANES // (RWKV_HEADS * vl)))


def _rwkv_context(r, k, v, w, a, k_k, k_a, B, T):
    n = RWKV_HEAD_DIM
    nb = B * RWKV_HEADS
    LG = -(-nb // LANES)

    def groups(x):
        f = _to_lanes(x, B, T)
        f = jnp.pad(f, [(0, 0)] * (f.ndim - 1) + [(0, LG * LANES - nb)])
        f = f.reshape(f.shape[:-1] + (LG, LANES))
        return jnp.moveaxis(f, -2, -4)

    s0 = jnp.zeros((2, LG, n, n, LANES), F32)
    o, sT = _rwkv_scan_groups(groups(r), groups(k), groups(v), groups(w), groups(a),
                              _lane_param(k_k, 1), _lane_param(k_a, 1), s0)
    o = jnp.moveaxis(o, 1, 3).reshape(2, T, n, LG * LANES)[..., :nb]
    o = o.reshape(2, T, n, B, RWKV_HEADS).transpose(0, 3, 1, 4, 2).reshape(2, B * T, BRANCH_WIDTH)
    s = jnp.moveaxis(sT, 1, 3).reshape(2, n, n, LG * LANES)[..., :nb]
    s = s.reshape(2, n, n, B, RWKV_HEADS).transpose(3, 0, 4, 2, 1)
    return o[0], o[1], s


def _rwkv_latent(r, k, v, w, a, k_k, k_a, state0, B, T):
    n = RWKV_HEAD_DIM
    H = RWKV_HEADS
    vl = min(SUBLANES, LANES // (2 * B * H))
    assert vl >= 1 and vl & (vl - 1) == 0
    nv = n // vl
    half = B * H * vl
    pad = lambda x: jnp.pad(x, [(0, 0)] * (x.ndim - 1) + [(0, LANES - 2 * half)])

    def k_lanes(x):
        return jnp.repeat(_to_lanes(x, B, T), vl, axis=-1)

    def shared_k(x):
        f = k_lanes(x)
        return pad(jnp.concatenate([f, f], axis=-1))

    def per_dir(x):
        f = k_lanes(x)
        return pad(jnp.concatenate([f[0], f[1]], axis=-1))

    vv = v.reshape(B, T, H, vl, nv).transpose(1, 4, 0, 2, 3).reshape(T, nv, half)
    vv = pad(jnp.concatenate([vv, vv], axis=-1))
    s0 = state0.reshape(B, 2, H, vl, nv, n).transpose(5, 4, 1, 0, 2, 3).reshape(n, nv, 2 * half)
    bwd = (jnp.arange(LANES) >= half).astype(F32)[None]
    o_f, o_b, _ = _rwkv_scan_lanes(shared_k(r), shared_k(k), vv, per_dir(w), per_dir(a), bwd,
                                   _lane_param(k_k, vl), _lane_param(k_a, vl), pad(s0))
    tok = lambda o: o.reshape(T, nv, B, H, vl).transpose(2, 0, 3, 4, 1).reshape(B * T, BRANCH_WIDTH)
    return tok(o_f[..., :half]), tok(o_b[..., half:2 * half])


def _merge_kernel(x_ref, mod_ref, gates_ref, oa_ref, ob_ref, of_ref, obw_ref, g_ref, bonus_ref,
                  gng_ref, gnb_ref, ones_ref, wup_ref, wout_ref, lng_ref, lnb_ref, o_ref, *, alpha):
    ones = ones_ref[...]
    inv_n = 1.0 / RWKV_HEAD_DIM
    o = of_ref[...] + obw_ref[...]
    mu = _seg_sum(o, ones) * inv_n
    d = o - mu
    var = _seg_sum(d * d, ones) * inv_n
    oc = d * lax.rsqrt(var + RWKV_GN_EPS) * gng_ref[...] + gnb_ref[...]
    oc = (oc + bonus_ref[...]) * g_ref[...]
    branches = (oa_ref[...], ob_ref[...], oc)
    mix = None
    for i in range(N_BRANCH):
        up = _dot(branches[i].astype(BF16), wup_ref[i])
        term = _sigmoid(gates_ref[:, i * D_MODEL:(i + 1) * D_MODEL]) * up
        mix = term if mix is None else mix + term
    h = _dot(mix.astype(BF16), wout_ref[...])
    x = x_ref[...]
    y = alpha * x + mod_ref[0, 5:6, :] * h
    o_ref[...] = _layer_norm(y, lng_ref[...], lnb_ref[...])


def _merge_sublayer(x, mod, gates, oa, ob, o_f, o_b, g, bonus, gn_g, gn_b, ones_bd, w_up, w_out, ln_g, ln_b,
                    *, alpha, row_map):
    n = x.shape[0]
    tm = ROW_TILE
    bw = BRANCH_WIDTH
    rows = lambda w: pl.BlockSpec((tm, w), lambda i: (i, 0))
    const = lambda shp: pl.BlockSpec(shp, lambda i: (0,) * len(shp))
    return pl.pallas_call(
        functools.partial(_merge_kernel, alpha=alpha),
        out_shape=jax.ShapeDtypeStruct((n, D_MODEL), F32),
        grid=(n // tm,),
        in_specs=[rows(D_MODEL), pl.BlockSpec((1, N_MOD, D_MODEL), lambda i: (row_map(i), 0, 0)),
                  rows(N_BRANCH * D_MODEL), rows(bw), rows(bw), rows(bw), rows(bw), rows(bw), rows(bw),
                  const((1, bw)), const((1, bw)), const((bw, bw)),
                  const((N_BRANCH, bw, D_MODEL)), const((D_MODEL, D_MODEL)),
                  const((1, D_MODEL)), const((1, D_MODEL))],
        out_specs=rows(D_MODEL),
        compiler_params=_cparams("arbitrary"),
        name="merge_sublayer",
    )(x, mod, gates, oa, ob, o_f, o_b, g, bonus, gn_g, gn_b, ones_bd, w_up, w_out,
      ln_g.reshape(1, D_MODEL), ln_b.reshape(1, D_MODEL))


def _heads_first(x, B, T, H):
    return x.reshape(B, T, H, x.shape[-1] // H).transpose(0, 2, 1, 3)


def kernel(x_prompt, x_sample, cache_attn_k, cache_attn_v, cache_diff_k, cache_diff_v, state_rwkv, c, c_ctx,
           w_mod, b_mod, ln_g, ln_b, ffn_w_in, ffn_w_out, w_in, qk_norm_g, diff_lambda, diff_subln_g,
           rwkv_mu, rwkv_w0, rwkv_w2, rwkv_a0, rwkv_a2, rwkv_g2, rwkv_k_k, rwkv_k_a, rwkv_r_k,
           rwkv_gn_g, rwkv_gn_b, w_up, w_out):
    Bp, Tp, _ = x_prompt.shape
    Bs, Ts, _ = x_sample.shape
    depth = w_mod.shape[0]
    alpha = (2.0 * depth) ** 0.25
    n_p, n_s = Bp * Tp, Bs * Ts
    assert n_p % ROW_TILE == 0 and Ts % ROW_TILE == 0 and 1 + Bs <= MOD_ROWS
    row_map = _mod_row_map(n_p // ROW_TILE, Ts // ROW_TILE)

    x = jnp.concatenate([x_prompt.reshape(n_p, D_MODEL), x_sample.reshape(n_s, D_MODEL)], axis=0)
    cvecs = jnp.concatenate([c_ctx[None], c, jnp.zeros((MOD_ROWS - 1 - Bs, D_MODEL), F32)], axis=0)
    mod_all = _modulation(cvecs, w_mod, b_mod)

    seg = np.arange(BRANCH_WIDTH) // HEAD_DIM
    ones_bd = jnp.asarray(seg[:, None] == seg[None, :], BF16)
    rope = _rope_tables(Ts)
    bw = BRANCH_WIDTH

    n_ak, n_av, n_dk, n_dv, n_st = [], [], [], [], []
    for l in range(depth):
        lam_init = 0.8 - 0.6 * math.exp(-0.3 * l)
        mod = mod_all[l]
        x = _ffn_sublayer(x, mod, ffn_w_in[l, 0].astype(BF16), ffn_w_out[l, 0].astype(BF16),
                          ln_g[l, 0], ln_b[l, 0], sub=0, alpha=alpha, row_map=row_map)
        gates, aq, ak, av, bq, bk, bv, cin = _in_projection(x, mod, w_in[l].astype(BF16), row_map=row_map)

        gq = jnp.tile(qk_norm_g[l, 0], ATTN_HEADS)[None]
        gk = jnp.tile(qk_norm_g[l, 1], ATTN_KV_HEADS)[None]
        oa_p, kn_p = _attn_a(aq, ak, av, gq, gk, ones_bd, B=Bp, T=Tp, row0=0, latent=False)
        oa_s, _ = _attn_a(aq, ak, av, gq, gk, ones_bd, B=Bs, T=Ts, row0=n_p, latent=True, rope=rope,
                          ctx_k=cache_attn_k[:, l], ctx_v=cache_attn_v[:, l])
        sg = diff_subln_g[l][None]
        ob_p = _attn_b(bq, bk, bv, diff_lambda[l], sg, B=Bp, T=Tp, row0=0, latent=False, lam_init=lam_init)
        ob_s = _attn_b(bq, bk, bv, diff_lambda[l], sg, B=Bs, T=Ts, row0=n_p, latent=True, lam_init=lam_init,
                       rope=rope, ctx_k=cache_diff_k[:, l], ctx_v=cache_diff_v[:, l])

        rp = {'mu': rwkv_mu[l][None], 'w0': rwkv_w0[l], 'w2': rwkv_w2[l].astype(BF16), 'a0': rwkv_a0[l],
              'a2': rwkv_a2[l].astype(BF16), 'g2': rwkv_g2[l].astype(BF16),
              'k_a': rwkv_k_a[l][None], 'r_k': rwkv_r_k[l].reshape(1, bw)}
        r, k, v, w, a, gg, bonus = _rwkv_prep(cin, rp, ones_bd, n_prompt=n_p, t_prompt=Tp, t_sample=Ts)
        sf_p, sb_p, st = _rwkv_context(r[:n_p], k[:n_p], v[:n_p], w[:, :n_p], a[:, :n_p],
                                       rwkv_k_k[l], rwkv_k_a[l], Bp, Tp)
        sf_s, sb_s = _rwkv_latent(r[n_p:], k[n_p:], v[n_p:], w[:, n_p:], a[:, n_p:],
                                  rwkv_k_k[l], rwkv_k_a[l], state_rwkv[:, l], Bs, Ts)
        n_st.append(st)
        cat = lambda parts: jnp.concatenate(parts, axis=0)
        x = _merge_sublayer(x, mod, gates, cat([oa_p, oa_s]), cat([ob_p, ob_s]), cat([sf_p, sf_s]),
                            cat([sb_p, sb_s]), gg, bonus, rwkv_gn_g[l][None], rwkv_gn_b[l][None], ones_bd,
                            w_up[l].astype(BF16), w_out[l].astype(BF16), ln_g[l, 1], ln_b[l, 1],
                            alpha=alpha, row_map=row_map)
        x = _ffn_sublayer(x, mod, ffn_w_in[l, 1].astype(BF16), ffn_w_out[l, 1].astype(BF16),
                          ln_g[l, 2], ln_b[l, 2], sub=2, alpha=alpha, row_map=row_map)

        n_ak.append(_heads_first(kn_p, Bp, Tp, ATTN_KV_HEADS))
        n_av.append(_heads_first(av[:n_p], Bp, Tp, ATTN_KV_HEADS))
        n_dk.append(_heads_first(bk[:n_p], Bp, Tp, DIFF_HEADS))
        n_dv.append(_heads_first(bv[:n_p], Bp, Tp, DIFF_HEADS))

    y_prompt = x[:n_p].reshape(Bp, Tp, D_MODEL)
    y_sample = x[n_p:].reshape(Bs, Ts, D_MODEL)
    stack = lambda parts: jnp.stack(parts, axis=1)
    return (y_prompt, y_sample, stack(n_ak), stack(n_av), stack(n_dk), stack(n_dv), stack(n_st))
```

```python
import functools
import math

import numpy as np
import jax
import jax.numpy as jnp
from jax import lax
from jax.experimental import pallas as pl
from jax.experimental.pallas import tpu as pltpu

F32 = jnp.float32
BF16 = jnp.bfloat16

D_MODEL = 1024
GRID_W = 64
HEAD_DIM = 64
ATTN_HEADS = 8
ATTN_KV_HEADS = 2
ATTN_GROUP = ATTN_HEADS // ATTN_KV_HEADS
DIFF_HEADS = 4
RWKV_HEADS = 8
RWKV_HEAD_DIM = 64
BRANCH_WIDTH = 512
W_LORA = 64
A_LORA = 64
G_LORA = 128
D_FF = 2816
N_BRANCH = 3
N_MOD = 9
ROPE_THETA = 10000.0
ROPE_AXIS_DIM = HEAD_DIM // 2
LN_EPS = 1e-5
RMS_EPS = 1e-6
RWKV_GN_EPS = 64e-5
A_Q_W = ATTN_HEADS * HEAD_DIM
A_KV_W = ATTN_KV_HEADS * HEAD_DIM
B_QK_W = DIFF_HEADS * 2 * HEAD_DIM
B_V_W = DIFF_HEADS * 2 * HEAD_DIM
C_IN_W = 3 * BRANCH_WIDTH + 2 * W_LORA + 2 * A_LORA + G_LORA
IN_SPLITS = (N_BRANCH * D_MODEL, A_Q_W, A_KV_W, A_KV_W, B_QK_W, B_QK_W, B_V_W, C_IN_W)
IN_WIDTH = sum(IN_SPLITS)

LANES = 128
SUBLANES = 8
VMEM_LIMIT_BYTES = 56 * 1024 * 1024

MOD_ROWS = 8
MOD_TN = 1152
FFN_CK = 256
ROW_TILE = 256
Q_TILE = 256
SCAN_TC = 32


def _cparams(*sem):
    return pltpu.CompilerParams(dimension_semantics=sem, vmem_limit_bytes=VMEM_LIMIT_BYTES)


def _sigmoid(x):
    return 1.0 / (1.0 + jnp.exp(-x))


def _dot(a, b):
    return jnp.dot(a, b, preferred_element_type=F32)


def _dot_t(a, b):
    return lax.dot_general(a, b, (((1,), (1,)), ((), ())), preferred_element_type=F32)


def _seg_sum(x, ones_bd):
    hi = x.astype(BF16)
    r1 = x - hi.astype(F32)
    mid = r1.astype(BF16)
    lo = (r1 - mid.astype(F32)).astype(BF16)
    return _dot(hi, ones_bd) + _dot(mid, ones_bd) + _dot(lo, ones_bd)


def _layer_norm(y, g, b):
    mu = jnp.mean(y, axis=-1, keepdims=True)
    d = y - mu
    var = jnp.mean(d * d, axis=-1, keepdims=True)
    return d * lax.rsqrt(var + LN_EPS) * g + b


def _mod_row_map(n_prompt_blocks, blocks_per_sample):
    def row(i):
        return jnp.where(i < n_prompt_blocks, 0, 1 + (i - n_prompt_blocks) // blocks_per_sample)
    return row


def _mod_kernel(c_ref, w_ref, b_ref, o_ref):
    cv = c_ref[...]
    h = (cv * _sigmoid(cv)).astype(BF16)
    o_ref[0] = _dot(h, w_ref[0].astype(BF16)) + b_ref[0]


def _modulation(cvecs, w_mod, b_mod):
    L = w_mod.shape[0]
    n = N_MOD * D_MODEL
    out = pl.pallas_call(
        _mod_kernel,
        out_shape=jax.ShapeDtypeStruct((L, MOD_ROWS, n), F32),
        grid=(L, n // MOD_TN),
        in_specs=[
            pl.BlockSpec((MOD_ROWS, D_MODEL), lambda l, j: (0, 0)),
            pl.BlockSpec((1, D_MODEL, MOD_TN), lambda l, j: (l, 0, j)),
            pl.BlockSpec((1, 1, MOD_TN), lambda l, j: (l, 0, j)),
        ],
        out_specs=pl.BlockSpec((1, MOD_ROWS, MOD_TN), lambda l, j: (l, 0, j)),
        compiler_params=_cparams("arbitrary", "arbitrary"),
        name="modulation",
    )(cvecs, w_mod, b_mod.reshape(L, 1, n))
    return out.reshape(L, MOD_ROWS, N_MOD, D_MODEL)


def _ffn_kernel(x_ref, mod_ref, win_ref, wout_ref, g_ref, b_ref, o_ref, act_ref, *, sub, alpha):
    x = x_ref[...]
    shift = mod_ref[0, 3 * sub:3 * sub + 1, :]
    scale = mod_ref[0, 3 * sub + 1:3 * sub + 2, :]
    gate = mod_ref[0, 3 * sub + 2:3 * sub + 3, :]
    h = (x * (1.0 + scale) + shift).astype(BF16)
    for c in range(D_FF // FFN_CK):
        a = _dot(h, win_ref[:, c * FFN_CK:(c + 1) * FFN_CK])
        b = _dot(h, win_ref[:, D_FF + c * FFN_CK:D_FF + (c + 1) * FFN_CK])
        act_ref[:, c * FFN_CK:(c + 1) * FFN_CK] = (a * _sigmoid(a) * b).astype(BF16)
    f = _dot(act_ref[...], wout_ref[...])
    y = alpha * x + (0.5 * gate) * f
    o_ref[...] = _layer_norm(y, g_ref[...], b_ref[...])


def _ffn_sublayer(x, mod, w_in, w_out, ln_g, ln_b, *, sub, alpha, row_map):
    n = x.shape[0]
    tm = ROW_TILE
    const = lambda i: (0, 0)
    return pl.pallas_call(
        functools.partial(_ffn_kernel, sub=sub, alpha=alpha),
        out_shape=jax.ShapeDtypeStruct((n, D_MODEL), F32),
        grid=(n // tm,),
        in_specs=[
            pl.BlockSpec((tm, D_MODEL), lambda i: (i, 0)),
            pl.BlockSpec((1, N_MOD, D_MODEL), lambda i: (row_map(i), 0, 0)),
            pl.BlockSpec((D_MODEL, 2 * D_FF), const),
            pl.BlockSpec((D_FF, D_MODEL), const),
            pl.BlockSpec((1, D_MODEL), const),
            pl.BlockSpec((1, D_MODEL), const),
        ],
        out_specs=pl.BlockSpec((tm, D_MODEL), lambda i: (i, 0)),
        scratch_shapes=[pltpu.VMEM((tm, D_FF), BF16)],
        compiler_params=_cparams("arbitrary"),
        name="ffn_sublayer",
    )(x, mod, w_in, w_out, ln_g.reshape(1, D_MODEL), ln_b.reshape(1, D_MODEL))


def _inproj_kernel(x_ref, mod_ref, w_ref, *out_refs):
    x = x_ref[...]
    shift = mod_ref[0, 3:4, :]
    scale = mod_ref[0, 4:5, :]
    h = (x * (1.0 + scale) + shift).astype(BF16)
    off = 0
    for o_ref, width in zip(out_refs, IN_SPLITS):
        o_ref[...] = _dot(h, w_ref[:, off:off + width])
        off += width


def _in_projection(x, mod, w_in, *, row_map):
    n = x.shape[0]
    tm = ROW_TILE
    return pl.pallas_call(
        _inproj_kernel,
        out_shape=tuple(jax.ShapeDtypeStruct((n, w), F32) for w in IN_SPLITS),
        grid=(n // tm,),
        in_specs=[
            pl.BlockSpec((tm, D_MODEL), lambda i: (i, 0)),
            pl.BlockSpec((1, N_MOD, D_MODEL), lambda i: (row_map(i), 0, 0)),
            pl.BlockSpec((D_MODEL, IN_WIDTH), lambda i: (0, 0)),
        ],
        out_specs=tuple(pl.BlockSpec((tm, w), lambda i: (i, 0)) for w in IN_SPLITS),
        compiler_params=_cparams("arbitrary"),
        name="in_projection",
    )(x, mod, w_in)


def _rope_tables(T):
    n_rows = T // GRID_W
    row = np.repeat(np.arange(n_rows), GRID_W).astype(np.float32)
    col = np.tile(np.arange(GRID_W), n_rows).astype(np.float32)
    inv = (1.0 / (ROPE_THETA ** (np.arange(0, ROPE_AXIS_DIM, 2, dtype=np.float32) / ROPE_AXIS_DIM))).astype(np.float32)
    ang_r = row[:, None] * inv
    ang_c = col[:, None] * inv
    z = np.zeros_like(ang_r)
    cos = np.concatenate([np.cos(ang_r), np.cos(ang_r), np.cos(ang_c), np.cos(ang_c)], axis=1)
    s_up = np.concatenate([-np.sin(ang_r), z, -np.sin(ang_c), z], axis=1)
    s_dn = np.concatenate([z, np.sin(ang_r), z, np.sin(ang_c)], axis=1)
    tile2 = lambda t: jnp.asarray(np.concatenate([t, t], axis=1), F32)
    return tile2(cos), tile2(s_up), tile2(s_dn)


def _rope128(x, cos, s_up, s_dn):
    up = pltpu.roll(x, LANES - ROPE_AXIS_DIM // 2, axis=1)
    dn = pltpu.roll(x, ROPE_AXIS_DIM // 2, axis=1)
    return x * cos + up * s_up + dn * s_dn


def _softmax_rows(s):
    m = jnp.max(s, axis=-1, keepdims=True)
    e = jnp.exp(s - m)
    return e / jnp.sum(e, axis=-1, keepdims=True)


def _attn_a_kernel(*refs, latent, T, past):
    o_ref, kn_ref, q_s, k_s, v_s = refs[-5:]
    if latent:
        aq_ref, ak_ref, av_ref, gq_ref, gk_ref, ones_ref, cos_ref, sup_ref, sdn_ref, ck_ref, cv_ref = refs[:11]
    else:
        aq_ref, ak_ref, av_ref, gq_ref, gk_ref, ones_ref = refs[:6]
    ones = ones_ref[...]
    inv_d = 1.0 / HEAD_DIM
    aq = aq_ref[...]
    qn = aq * lax.rsqrt(_seg_sum(aq * aq, ones) * inv_d + RMS_EPS) * gq_ref[...]
    ak = ak_ref[...]
    kn = ak * lax.rsqrt(_seg_sum(ak * ak, ones[:A_KV_W, :A_KV_W]) * inv_d + RMS_EPS) * gk_ref[...]
    kn_ref[...] = kn
    if latent:
        cos, sup, sdn = cos_ref[...], sup_ref[...], sdn_ref[...]
        for j in range(A_Q_W // LANES):
            q_s[:, j * LANES:(j + 1) * LANES] = _rope128(
                qn[:, j * LANES:(j + 1) * LANES], cos, sup, sdn).astype(BF16)
        kr = _rope128(kn, cos, sup, sdn).astype(BF16)
    else:
        q_s[...] = qn.astype(BF16)
        kr = kn.astype(BF16)
    av = av_ref[...].astype(BF16)
    for h in range(ATTN_KV_HEADS):
        k_s[h, 0:T, :] = kr[:, h * HEAD_DIM:(h + 1) * HEAD_DIM]
        v_s[h, 0:T, :] = av[:, h * HEAD_DIM:(h + 1) * HEAD_DIM]
        if latent:
            k_s[h, T:T + past, :] = ck_ref[0, h].astype(BF16)
            v_s[h, T:T + past, :] = cv_ref[0, h].astype(BF16)
    scale = HEAD_DIM ** -0.5
    for qb in range(T // Q_TILE):
        rows = slice(qb * Q_TILE, (qb + 1) * Q_TILE)
        outs = []
        for j in range(ATTN_HEADS):
            h = j // ATTN_GROUP
            q = q_s[rows, j * HEAD_DIM:(j + 1) * HEAD_DIM]
            s = _dot_t(q, k_s[h]) * scale
            p = _softmax_rows(s).astype(BF16)
            outs.append(_dot(p, v_s[h]))
        o_ref[rows, :] = jnp.concatenate(outs, axis=-1)


def _attn_a(aq, ak, av, gq, gk, ones_bd, *, B, T, row0, latent, rope=None, ctx_k=None, ctx_v=None, shared=None):
    blk0 = row0 // T
    n = aq.shape[0]
    past = ctx_k.shape[2] if latent else 0
    rows = lambda w: pl.BlockSpec((T, w), lambda b: (blk0 + b, 0))
    const = lambda shp: pl.BlockSpec(shp, lambda b: (0,) * len(shp))
    in_specs = [rows(A_Q_W), rows(A_KV_W), rows(A_KV_W), const((1, A_Q_W)), const((1, A_KV_W)),
                const((A_Q_W, A_Q_W))]
    args = [aq, ak, av, gq, gk, ones_bd]
    if latent:
        in_specs += [const((T, LANES))] * 3
        in_specs += [pl.BlockSpec((1, ATTN_KV_HEADS, past, HEAD_DIM), lambda b: (b, 0, 0, 0))] * 2
        args += list(rope) + [ctx_k, ctx_v]
    aliases = {}
    if shared is not None:
        aliases = {len(args): 0}
        in_specs.append(pl.BlockSpec(memory_space=pl.ANY))
        args.append(shared)
    return pl.pallas_call(
        functools.partial(_attn_a_kernel, latent=latent, T=T, past=past),
        out_shape=(jax.ShapeDtypeStruct((n, A_Q_W), F32), jax.ShapeDtypeStruct((B * T, A_KV_W), F32)),
        grid=(B,),
        in_specs=in_specs,
        out_specs=(pl.BlockSpec((T, A_Q_W), lambda b: (blk0 + b, 0)), pl.BlockSpec((T, A_KV_W), lambda b: (b, 0))),
        input_output_aliases=aliases,
        scratch_shapes=[pltpu.VMEM((T, A_Q_W), BF16),
                        pltpu.VMEM((ATTN_KV_HEADS, T + past, HEAD_DIM), BF16),
                        pltpu.VMEM((ATTN_KV_HEADS, T + past, HEAD_DIM), BF16)],
        compiler_params=_cparams("arbitrary"),
        name="attn_gqa_latent" if latent else "attn_gqa_context",
    )(*args)


def _attn_b_kernel(*refs, latent, T, past, lam_init):
    o_ref, k_s, v_s = refs[-3:]
    if latent:
        bq_ref, bk_ref, bv_ref, lv_ref, sg_ref, cos_ref, sup_ref, sdn_ref, ck_ref, cv_ref = refs[:10]
    else:
        bq_ref, bk_ref, bv_ref, lv_ref, sg_ref = refs[:5]
    lv = lv_ref[...]
    d01 = jnp.sum(lv[0:1, :] * lv[1:2, :], axis=-1, keepdims=True)
    d23 = jnp.sum(lv[2:3, :] * lv[3:4, :], axis=-1, keepdims=True)
    lam = jnp.exp(d01) - jnp.exp(d23) + lam_init
    q = bq_ref[...]
    k = bk_ref[...]
    if latent:
        cos, sup, sdn = cos_ref[...], sup_ref[...], sdn_ref[...]
        q = _rope128(q, cos, sup, sdn)
        k = _rope128(k, cos, sup, sdn)
    q = q.astype(BF16)
    k_s[0:T, :] = k.astype(BF16)
    v_s[0:T, :] = bv_ref[...].astype(BF16)
    if latent:
        k_s[T:T + past, :] = ck_ref[0, 0].astype(BF16)
        v_s[T:T + past, :] = cv_ref[0, 0].astype(BF16)
    scale = HEAD_DIM ** -0.5
    inv_d = 1.0 / (2 * HEAD_DIM)
    k1 = k_s[:, 0:HEAD_DIM]
    k2 = k_s[:, HEAD_DIM:2 * HEAD_DIM]
    v = v_s[...]
    for qb in range(T // Q_TILE):
        rows = slice(qb * Q_TILE, (qb + 1) * Q_TILE)
        s1 = _dot_t(q[rows, 0:HEAD_DIM], k1) * scale
        s2 = _dot_t(q[rows, HEAD_DIM:2 * HEAD_DIM], k2) * scale
        p = (_softmax_rows(s1) - lam * _softmax_rows(s2)).astype(BF16)
        o = _dot(p, v)
        ms = jnp.mean(o * o, axis=-1, keepdims=True)
        o_ref[rows, :] = o * lax.rsqrt(ms + RMS_EPS) * sg_ref[...] * (1.0 - lam_init)


def _attn_b(bq, bk, bv, lam_vec, subln_g, *, B, T, row0, latent, lam_init, rope=None, ctx_k=None, ctx_v=None,
            shared=None):
    blk0 = row0 // T
    n = bq.shape[0]
    past = ctx_k.shape[2] if latent else 0
    dh = 2 * HEAD_DIM
    rows = pl.BlockSpec((T, dh), lambda b, h: (blk0 + b, h))
    const = lambda shp: pl.BlockSpec(shp, lambda b, h: (0,) * len(shp))
    in_specs = [rows, rows, rows, const((4, HEAD_DIM)), const((1, dh))]
    args = [bq, bk, bv, lam_vec, subln_g]
    if latent:
        in_specs += [const((T, LANES))] * 3
        in_specs += [pl.BlockSpec((1, 1, past, dh), lambda b, h: (b, h, 0, 0))] * 2
        args += list(rope) + [ctx_k, ctx_v]
    aliases = {}
    if shared is not None:
        aliases = {len(args): 0}
        in_specs.append(pl.BlockSpec(memory_space=pl.ANY))
        args.append(shared)
    return pl.pallas_call(
        functools.partial(_attn_b_kernel, latent=latent, T=T, past=past, lam_init=lam_init),
        out_shape=jax.ShapeDtypeStruct((n, B_V_W), F32),
        grid=(B, DIFF_HEADS),
        in_specs=in_specs,
        out_specs=pl.BlockSpec((T, dh), lambda b, h: (blk0 + b, h)),
        input_output_aliases=aliases,
        scratch_shapes=[pltpu.VMEM((T + past, dh), BF16), pltpu.VMEM((T + past, dh), BF16)],
        compiler_params=_cparams("arbitrary", "arbitrary"),
        name="attn_diff_latent" if latent else "attn_diff_context",
    )(*args)


def _rwkv_prep_kernel(cin_ref, prev_ref, next_ref, mu_ref, w0_ref, w2_ref, a0_ref, a2_ref, g2_ref,
                      ka_ref, rk_ref, ones_ref, *rest, tm, per_seq):
    r_o, k_o, v_o, w_o, a_o, g_o, bonus_o = rest[-7:]
    pos = pl.program_id(0) % per_seq
    row_before = jnp.where(pos == 0, 0.0, prev_ref[SUBLANES - 1:SUBLANES, :])
    row_after = jnp.where(pos == per_seq - 1, 0.0, next_ref[0:1, :])
    x = cin_ref[...]
    t_idx = lax.broadcasted_iota(jnp.int32, (tm, 1), 0)
    prev = jnp.where(t_idx == 0, row_before, pltpu.roll(x, 1, axis=0))
    nxt = jnp.where(t_idx == tm - 1, row_after, pltpu.roll(x, tm - 1, axis=0))
    x = x + mu_ref[...] * (0.5 * (prev + nxt) - x)
    bw = BRANCH_WIDTH
    r = x[:, 0:bw]
    k = x[:, bw:2 * bw]
    v = x[:, 2 * bw:3 * bw]
    off = 3 * bw
    wl = x[:, off:off + 2 * W_LORA]
    al = x[:, off + 2 * W_LORA:off + 2 * W_LORA + 2 * A_LORA]
    gl = x[:, off + 2 * W_LORA + 2 * A_LORA:]
    ones = ones_ref[...]
    r_o[...] = r
    k_o[...] = k
    v_o[...] = v
    g_o[...] = _dot(_sigmoid(gl).astype(BF16), g2_ref[...])
    tw = jnp.tanh(wl).astype(BF16)
    alb = al.astype(BF16)
    decay_rate = math.exp(-0.5)
    bonus = jnp.zeros_like(r)
    for d in range(2):
        w_logit = w0_ref[d:d + 1, :] + _dot(tw[:, d * W_LORA:(d + 1) * W_LORA], w2_ref[d])
        w_o[d] = jnp.exp(-decay_rate * _sigmoid(w_logit))
        a = _sigmoid(a0_ref[d:d + 1, :] + _dot(alb[:, d * A_LORA:(d + 1) * A_LORA], a2_ref[d]))
        a_o[d] = a
        kc = k * (1.0 + (a - 1.0) * ka_ref[...])
        bonus = bonus + _seg_sum(r * kc * rk_ref[...], ones) * v
    bonus_o[...] = bonus


def _rwkv_prep(cin, p, ones_bd, *, row0, n_rows, t_seq, shared=None):
    n = cin.shape[0]
    tm = ROW_TILE
    bw = BRANCH_WIDTH
    blk0 = row0 // tm
    halo = tm // SUBLANES
    last = n // SUBLANES - 1
    const = lambda shp: pl.BlockSpec(shp, lambda i: (0,) * len(shp))
    tok = jax.ShapeDtypeStruct((n_rows, bw), F32)
    tok2 = jax.ShapeDtypeStruct((2, n_rows, bw), F32)
    full = jax.ShapeDtypeStruct((n, bw), F32)
    tspec = pl.BlockSpec((tm, bw), lambda i: (i, 0))
    tspec2 = pl.BlockSpec((2, tm, bw), lambda i: (0, i, 0))
    fspec = pl.BlockSpec((tm, bw), lambda i: (blk0 + i, 0))
    in_specs = [pl.BlockSpec((tm, C_IN_W), lambda i: (blk0 + i, 0)),
                pl.BlockSpec((SUBLANES, C_IN_W), lambda i: (jnp.maximum((blk0 + i) * halo - 1, 0), 0)),
                pl.BlockSpec((SUBLANES, C_IN_W), lambda i: (jnp.minimum((blk0 + i + 1) * halo, last), 0)),
                const((1, C_IN_W)), const((2, bw)), const((2, W_LORA, bw)), const((2, bw)),
                const((2, A_LORA, bw)), const((G_LORA, bw)), const((1, bw)), const((1, bw)),
                const((bw, bw))]
    args = [cin, cin, cin, p['mu'], p['w0'], p['w2'], p['a0'], p['a2'], p['g2'], p['k_a'], p['r_k'], ones_bd]
    aliases = {}
    if shared is not None:
        aliases = {len(args): 5, len(args) + 1: 6}
        in_specs += [pl.BlockSpec(memory_space=pl.ANY)] * 2
        args += list(shared)
    return pl.pallas_call(
        functools.partial(_rwkv_prep_kernel, tm=tm, per_seq=t_seq // tm),
        out_shape=(tok, tok, tok, tok2, tok2, full, full),
        grid=(n_rows // tm,),
        in_specs=in_specs,
        out_specs=(tspec, tspec, tspec, tspec2, tspec2, fspec, fspec),
        input_output_aliases=aliases,
        compiler_params=_cparams("arbitrary"),
        name="rwkv_streams",
    )(*args)


def _scan_kernel(*refs, tc, nv, dir_lanes):
    if dir_lanes:
        (rf, kf, vf, wf, af, rb, kb, vb, wb, ab, bwd_ref, kk_ref, ka_ref, s0_ref,
         of_ref, ob_ref, sT_ref, S, al_s, wr_s, be_s, kc_s, w_s) = refs
        bwd = bwd_ref[...] > 0.0
    else:
        (rf, kf, vf, wf, af, kk_ref, ka_ref, s0_ref, of_ref, sT_ref, S, al_s, wr_s, be_s, kc_s, w_s) = refs
    d = pl.program_id(0)
    j = pl.program_id(2)
    n = RWKV_HEAD_DIM

    @pl.when(j == 0)
    def _():
        S[...] = s0_ref[...]

    def step(i, carry):
        if dir_lanes:
            ib = tc - 1 - i
            r = jnp.where(bwd, rb[ib], rf[i])
            k = jnp.where(bwd, kb[ib], kf[i])
            v = jnp.where(bwd, vb[ib], vf[i])
            w = jnp.where(bwd, wb[ib], wf[i])
            a = jnp.where(bwd, ab[ib], af[i])
        else:
            row = i + d * (tc - 1 - 2 * i)
            r, k, v, w, a = rf[row], kf[row], vf[row], wf[row], af[row]
        kkv = k * kk_ref[...]
        kk = kkv * lax.rsqrt(jnp.sum(kkv * kkv, axis=0, keepdims=True) + 1e-12)
        kc = k * (1.0 + (a - 1.0) * ka_ref[...])
        beta = kk * a
        al_s[...] = -kk
        wr_s[...] = w * r
        be_s[...] = beta
        kc_s[...] = kc
        w_s[...] = w
        c1 = jnp.sum(beta * r, axis=0, keepdims=True)
        c2 = jnp.sum(kc * r, axis=0, keepdims=True)

        def body_a(kx, acc):
            sa, y = acc
            sk = S[kx]
            return sa + sk * al_s[pl.ds(kx, 1), :], y + sk * wr_s[pl.ds(kx, 1), :]

        zero = jnp.zeros((nv, LANES), F32)
        sa, y = lax.fori_loop(0, n, body_a, (zero, zero), unroll=True)

        def body_b(kx, c):
            S[kx] = S[kx] * w_s[pl.ds(kx, 1), :] + be_s[pl.ds(kx, 1), :] * sa + kc_s[pl.ds(kx, 1), :] * v
            return c

        lax.fori_loop(0, n, body_b, 0, unroll=True)
        out = y + sa * c1 + v * c2
        if dir_lanes:
            of_ref[i] = out
            ob_ref[ib] = out
        else:
            of_ref[row] = out
        return carry

    lax.fori_loop(0, tc, step, 0)

    @pl.when(j == pl.num_programs(2) - 1)
    def _():
        sT_ref[...] = S[...]


def _scan_scratch(nv):
    n = RWKV_HEAD_DIM
    return [pltpu.VMEM((n, nv, LANES), F32)] + [pltpu.VMEM((n, LANES), F32)] * 5


def _rwkv_scan_groups(r, k, v, w, a, kk_t, ka_t, s0):
    LG, T = r.shape[:2]
    n = RWKV_HEAD_DIM
    tc = SCAN_TC
    nt = T // tc
    tb = lambda d, j: j + d * (nt - 1 - 2 * j)
    shared = pl.BlockSpec((None, tc, n, LANES), lambda d, g, j: (g, tb(d, j), 0, 0))
    perdir = pl.BlockSpec((None, None, tc, n, LANES), lambda d, g, j: (d, g, tb(d, j), 0, 0))
    par = pl.BlockSpec((n, LANES), lambda d, g, j: (0, 0))
    st = pl.BlockSpec((None, None, n, n, LANES), lambda d, g, j: (d, g, 0, 0, 0))
    return pl.pallas_call(
        functools.partial(_scan_kernel, tc=tc, nv=n, dir_lanes=False),
        out_shape=(jax.ShapeDtypeStruct((2, LG, T, n, LANES), F32), jax.ShapeDtypeStruct((2, LG, n, n, LANES), F32)),
        grid=(2, LG, nt),
        in_specs=[shared, shared, shared, perdir, perdir, par, par, st],
        out_specs=(perdir, st),
        scratch_shapes=_scan_scratch(n),
        compiler_params=_cparams("arbitrary", "arbitrary", "arbitrary"),
        name="rwkv_scan_context",
    )(r, k, v, w, a, kk_t, ka_t, s0)


def _rwkv_scan_lanes(r, k, v, w, a, bwd, kk_t, ka_t, s0):
    T = r.shape[0]
    nv = v.shape[1]
    n = RWKV_HEAD_DIM
    tc = SCAN_TC
    nt = T // tc
    fwd_k = pl.BlockSpec((tc, n, LANES), lambda d, g, j: (j, 0, 0))
    bwd_k = pl.BlockSpec((tc, n, LANES), lambda d, g, j: (nt - 1 - j, 0, 0))
    fwd_v = pl.BlockSpec((tc, nv, LANES), lambda d, g, j: (j, 0, 0))
    bwd_v = pl.BlockSpec((tc, nv, LANES), lambda d, g, j: (nt - 1 - j, 0, 0))
    par = pl.BlockSpec((n, LANES), lambda d, g, j: (0, 0))
    st = pl.BlockSpec((n, nv, LANES), lambda d, g, j: (0, 0, 0))
    o_sds = jax.ShapeDtypeStruct((T, nv, LANES), F32)
    return pl.pallas_call(
        functools.partial(_scan_kernel, tc=tc, nv=nv, dir_lanes=True),
        out_shape=(o_sds, o_sds, jax.ShapeDtypeStruct((n, nv, LANES), F32)),
        grid=(1, 1, nt),
        in_specs=[fwd_k, fwd_k, fwd_v, fwd_k, fwd_k, bwd_k, bwd_k, bwd_v, bwd_k, bwd_k,
                  pl.BlockSpec((1, LANES), lambda d, g, j: (0, 0)), par, par, st],
        out_specs=(fwd_v, bwd_v, st),
        scratch_shapes=_scan_scratch(nv),
        compiler_params=_cparams("arbitrary", "arbitrary", "arbitrary"),
        name="rwkv_scan_latent",
    )(r, k, v, w, a, r, k, v, w, a, bwd, kk_t, ka_t, s0)


def _to_lanes(x, B, T):
    lead = x.shape[:-2]
    x = x.reshape(lead + (B, T, RWKV_HEADS, RWKV_HEAD_DIM))
    nl = len(lead)
    perm = tuple(range(nl)) + (nl + 1, nl + 3, nl + 0, nl + 2)
    return x.transpose(perm).reshape(lead + (T, RWKV_HEAD_DIM, B * RWKV_HEADS))


def _lane_param(p, vl):
    t = jnp.repeat(p.reshape(RWKV_HEADS, RWKV_HEAD_DIM).T, vl, axis=1)
    return jnp.tile(t, (1, LANES // (RWKV_HEADS * vl)))


def _rwkv_context(r, k, v, w, a, k_k, k_a, B, T):
    n = RWKV_HEAD_DIM
    nb = B * RWKV_HEADS
    LG = -(-nb // LANES)

    def groups(x):
        f = _to_lanes(x, B, T)
        f = jnp.pad(f, [(0, 0)] * (f.ndim - 1) + [(0, LG * LANES - nb)])
        f = f.reshape(f.shape[:-1] + (LG, LANES))
        return jnp.moveaxis(f, -2, -4)

    s0 = jnp.zeros((2, LG, n, n, LANES), F32)
    o, sT = _rwkv_scan_groups(groups(r), groups(k), groups(v), groups(w), groups(a),
                              _lane_param(k_k, 1), _lane_param(k_a, 1), s0)
    o = jnp.moveaxis(o, 1, 3).reshape(2, T, n, LG * LANES)[..., :nb]
    o = o.reshape(2, T, n, B, RWKV_HEADS).transpose(0, 3, 1, 4, 2).reshape(2, B * T, BRANCH_WIDTH)
    s = jnp.moveaxis(sT, 1, 3).reshape(2, n, n, LG * LANES)[..., :nb]
    s = s.reshape(2, n, n, B, RWKV_HEADS).transpose(3, 0, 4, 2, 1)
    return o[0], o[1], s


def _rwkv_latent(r, k, v, w, a, k_k, k_a, state0, B, T):
    n = RWKV_HEAD_DIM
    H = RWKV_HEADS
    vl = min(SUBLANES, LANES // (2 * B * H))
    assert vl >= 1 and vl & (vl - 1) == 0
    nv = n // vl
    half = B * H * vl
    pad = lambda x: jnp.pad(x, [(0, 0)] * (x.ndim - 1) + [(0, LANES - 2 * half)])

    def k_lanes(x):
        return jnp.repeat(_to_lanes(x, B, T), vl, axis=-1)

    def shared_k(x):
        f = k_lanes(x)
        return pad(jnp.concatenate([f, f], axis=-1))

    def per_dir(x):
        f = k_lanes(x)
        return pad(jnp.concatenate([f[0], f[1]], axis=-1))

    vv = v.reshape(B, T, H, vl, nv).transpose(1, 4, 0, 2, 3).reshape(T, nv, half)
    vv = pad(jnp.concatenate([vv, vv], axis=-1))
    s0 = state0.reshape(B, 2, H, vl, nv, n).transpose(5, 4, 1, 0, 2, 3).reshape(n, nv, 2 * half)
    bwd = (jnp.arange(LANES) >= half).astype(F32)[None]
    o_f, o_b, _ = _rwkv_scan_lanes(shared_k(r), shared_k(k), vv, per_dir(w), per_dir(a), bwd,
                                   _lane_param(k_k, vl), _lane_param(k_a, vl), pad(s0))
    tok = lambda o: o.reshape(T, nv, B, H, vl).transpose(2, 0, 3, 4, 1).reshape(B * T, BRANCH_WIDTH)
    return tok(o_f[..., :half]), tok(o_b[..., half:2 * half])


def _merge_kernel(x_ref, mod_ref, gates_ref, oa_ref, ob_ref, sfp_ref, sbp_ref, sfs_ref, sbs_ref, g_ref, bonus_ref,
                  gng_ref, gnb_ref, ones_ref, wup_ref, wout_ref, lng_ref, lnb_ref, o_ref,
                  *, alpha, n_prompt_blocks):
    ones = ones_ref[...]
    inv_n = 1.0 / RWKV_HEAD_DIM
    in_prompt = pl.program_id(0) < n_prompt_blocks
    o = jnp.where(in_prompt, sfp_ref[...] + sbp_ref[...], sfs_ref[...] + sbs_ref[...])
    mu = _seg_sum(o, ones) * inv_n
    d = o - mu
    var = _seg_sum(d * d, ones) * inv_n
    oc = d * lax.rsqrt(var + RWKV_GN_EPS) * gng_ref[...] + gnb_ref[...]
    oc = (oc + bonus_ref[...]) * g_ref[...]
    branches = (oa_ref[...], ob_ref[...], oc)
    mix = None
    for i in range(N_BRANCH):
        up = _dot(branches[i].astype(BF16), wup_ref[i])
        term = _sigmoid(gates_ref[:, i * D_MODEL:(i + 1) * D_MODEL]) * up
        mix = term if mix is None else mix + term
    h = _dot(mix.astype(BF16), wout_ref[...])
    x = x_ref[...]
    y = alpha * x + mod_ref[0, 5:6, :] * h
    o_ref[...] = _layer_norm(y, lng_ref[...], lnb_ref[...])


def _merge_sublayer(x, mod, gates, oa, ob, sf_p, sb_p, sf_s, sb_s, g, bonus, gn_g, gn_b, ones_bd, w_up, w_out,
                    ln_g, ln_b, *, alpha, row_map):
    n = x.shape[0]
    tm = ROW_TILE
    bw = BRANCH_WIDTH
    npb = sf_p.shape[0] // tm
    rows = lambda w: pl.BlockSpec((tm, w), lambda i: (i, 0))
    prow = pl.BlockSpec((tm, bw), lambda i: (jnp.minimum(i, npb - 1), 0))
    srow = pl.BlockSpec((tm, bw), lambda i: (jnp.maximum(i - npb, 0), 0))
    const = lambda shp: pl.BlockSpec(shp, lambda i: (0,) * len(shp))
    return pl.pallas_call(
        functools.partial(_merge_kernel, alpha=alpha, n_prompt_blocks=npb),
        out_shape=jax.ShapeDtypeStruct((n, D_MODEL), F32),
        grid=(n // tm,),
        in_specs=[rows(D_MODEL), pl.BlockSpec((1, N_MOD, D_MODEL), lambda i: (row_map(i), 0, 0)),
                  rows(N_BRANCH * D_MODEL), rows(bw), rows(bw), prow, prow, srow, srow, rows(bw), rows(bw),
                  const((1, bw)), const((1, bw)), const((bw, bw)),
                  const((N_BRANCH, bw, D_MODEL)), const((D_MODEL, D_MODEL)),
                  const((1, D_MODEL)), const((1, D_MODEL))],
        out_specs=rows(D_MODEL),
        compiler_params=_cparams("arbitrary"),
        name="merge_sublayer",
    )(x, mod, gates, oa, ob, sf_p, sb_p, sf_s, sb_s, g, bonus, gn_g, gn_b, ones_bd, w_up, w_out,
      ln_g.reshape(1, D_MODEL), ln_b.reshape(1, D_MODEL))


def _heads_first(x, B, T, H):
    return x.reshape(B, T, H, x.shape[-1] // H).transpose(0, 2, 1, 3)


def kernel(x_prompt, x_sample, cache_attn_k, cache_attn_v, cache_diff_k, cache_diff_v, state_rwkv, c, c_ctx,
           w_mod, b_mod, ln_g, ln_b, ffn_w_in, ffn_w_out, w_in, qk_norm_g, diff_lambda, diff_subln_g,
           rwkv_mu, rwkv_w0, rwkv_w2, rwkv_a0, rwkv_a2, rwkv_g2, rwkv_k_k, rwkv_k_a, rwkv_r_k,
           rwkv_gn_g, rwkv_gn_b, w_up, w_out):
    Bp, Tp, _ = x_prompt.shape
    Bs, Ts, _ = x_sample.shape
    depth = w_mod.shape[0]
    alpha = (2.0 * depth) ** 0.25
    n_p, n_s = Bp * Tp, Bs * Ts
    assert n_p % ROW_TILE == 0 and Ts % ROW_TILE == 0 and 1 + Bs <= MOD_ROWS
    row_map = _mod_row_map(n_p // ROW_TILE, Ts // ROW_TILE)

    x = jnp.concatenate([x_prompt.reshape(n_p, D_MODEL), x_sample.reshape(n_s, D_MODEL)], axis=0)
    cvecs = jnp.concatenate([c_ctx[None], c, jnp.zeros((MOD_ROWS - 1 - Bs, D_MODEL), F32)], axis=0)
    mod_all = _modulation(cvecs, w_mod, b_mod)

    seg = np.arange(BRANCH_WIDTH) // HEAD_DIM
    ones_bd = jnp.asarray(seg[:, None] == seg[None, :], BF16)
    rope = _rope_tables(Ts)
    bw = BRANCH_WIDTH

    n_ak, n_av, n_dk, n_dv, n_st = [], [], [], [], []
    for l in range(depth):
        lam_init = 0.8 - 0.6 * math.exp(-0.3 * l)
        mod = mod_all[l]
        x = _ffn_sublayer(x, mod, ffn_w_in[l, 0].astype(BF16), ffn_w_out[l, 0].astype(BF16),
                          ln_g[l, 0], ln_b[l, 0], sub=0, alpha=alpha, row_map=row_map)
        gates, aq, ak, av, bq, bk, bv, cin = _in_projection(x, mod, w_in[l].astype(BF16), row_map=row_map)

        gq = jnp.tile(qk_norm_g[l, 0], ATTN_HEADS)[None]
        gk = jnp.tile(qk_norm_g[l, 1], ATTN_KV_HEADS)[None]
        oa_p, kn_p = _attn_a(aq, ak, av, gq, gk, ones_bd, B=Bp, T=Tp, row0=0, latent=False)
        oa, _ = _attn_a(aq, ak, av, gq, gk, ones_bd, B=Bs, T=Ts, row0=n_p, latent=True, rope=rope,
                        ctx_k=cache_attn_k[:, l], ctx_v=cache_attn_v[:, l], shared=oa_p)
        sg = diff_subln_g[l][None]
        ob_p = _attn_b(bq, bk, bv, diff_lambda[l], sg, B=Bp, T=Tp, row0=0, latent=False, lam_init=lam_init)
        ob = _attn_b(bq, bk, bv, diff_lambda[l], sg, B=Bs, T=Ts, row0=n_p, latent=True, lam_init=lam_init,
                     rope=rope, ctx_k=cache_diff_k[:, l], ctx_v=cache_diff_v[:, l], shared=ob_p)

        rp = {'mu': rwkv_mu[l][None], 'w0': rwkv_w0[l], 'w2': rwkv_w2[l].astype(BF16), 'a0': rwkv_a0[l],
              'a2': rwkv_a2[l].astype(BF16), 'g2': rwkv_g2[l].astype(BF16),
              'k_a': rwkv_k_a[l][None], 'r_k': rwkv_r_k[l].reshape(1, bw)}
        r, k, v, w, a, gg, bonus = _rwkv_prep(cin, rp, ones_bd, row0=0, n_rows=n_p, t_seq=Tp)
        sf_p, sb_p, st = _rwkv_context(r, k, v, w, a, rwkv_k_k[l], rwkv_k_a[l], Bp, Tp)
        r, k, v, w, a, gg, bonus = _rwkv_prep(cin, rp, ones_bd, row0=n_p, n_rows=n_s, t_seq=Ts,
                                              shared=(gg, bonus))
        sf_s, sb_s = _rwkv_latent(r, k, v, w, a, rwkv_k_k[l], rwkv_k_a[l], state_rwkv[:, l], Bs, Ts)
        n_st.append(st)
        x = _merge_sublayer(x, mod, gates, oa, ob, sf_p, sb_p, sf_s, sb_s, gg, bonus,
                            rwkv_gn_g[l][None], rwkv_gn_b[l][None], ones_bd,
                            w_up[l].astype(BF16), w_out[l].astype(BF16), ln_g[l, 1], ln_b[l, 1],
                            alpha=alpha, row_map=row_map)
        x = _ffn_sublayer(x, mod, ffn_w_in[l, 1].astype(BF16), ffn_w_out[l, 1].astype(BF16),
                          ln_g[l, 2], ln_b[l, 2], sub=2, alpha=alpha, row_map=row_map)

        n_ak.append(_heads_first(kn_p, Bp, Tp, ATTN_KV_HEADS))
        n_av.append(_heads_first(av[:n_p], Bp, Tp, ATTN_KV_HEADS))
        n_dk.append(_heads_first(bk[:n_p], Bp, Tp, DIFF_HEADS))
        n_dv.append(_heads_first(bv[:n_p], Bp, Tp, DIFF_HEADS))

    y_prompt = x[:n_p].reshape(Bp, Tp, D_MODEL)
    y_sample = x[n_p:].reshape(Bs, Ts, D_MODEL)
    stack = lambda parts: jnp.stack(parts, axis=1)
    return (y_prompt, y_sample, stack(n_ak), stack(n_av), stack(n_dk), stack(n_dv), stack(n_st))
```

```python
import functools
import math

import numpy as np
import jax
import jax.numpy as jnp
from jax import lax
from jax.experimental import pallas as pl
from jax.experimental.pallas import tpu as pltpu

F32 = jnp.float32
BF16 = jnp.bfloat16

D_MODEL = 1024
GRID_W = 64
HEAD_DIM = 64
ATTN_HEADS = 8
ATTN_KV_HEADS = 2
ATTN_GROUP = ATTN_HEADS // ATTN_KV_HEADS
DIFF_HEADS = 4
RWKV_HEADS = 8
RWKV_HEAD_DIM = 64
BRANCH_WIDTH = 512
W_LORA = 64
A_LORA = 64
G_LORA = 128
D_FF = 2816
N_BRANCH = 3
N_MOD = 9
ROPE_THETA = 10000.0
ROPE_AXIS_DIM = HEAD_DIM // 2
LN_EPS = 1e-5
RMS_EPS = 1e-6
RWKV_GN_EPS = 64e-5
A_Q_W = ATTN_HEADS * HEAD_DIM
A_KV_W = ATTN_KV_HEADS * HEAD_DIM
B_QK_W = DIFF_HEADS * 2 * HEAD_DIM
B_V_W = DIFF_HEADS * 2 * HEAD_DIM
C_IN_W = 3 * BRANCH_WIDTH + 2 * W_LORA + 2 * A_LORA + G_LORA
IN_SPLITS = (N_BRANCH * D_MODEL, A_Q_W, A_KV_W, A_KV_W, B_QK_W, B_QK_W, B_V_W, C_IN_W)
IN_WIDTH = sum(IN_SPLITS)

LANES = 128
SUBLANES = 8
VMEM_LIMIT_BYTES = 56 * 1024 * 1024

MOD_ROWS = 8
MOD_TN = 1152
FFN_CK = 256
ROW_TILE = 256
Q_TILE = 256
SCAN_TC = 32
SCAN_UNROLL = 8


def _cparams(*sem):
    return pltpu.CompilerParams(dimension_semantics=sem, vmem_limit_bytes=VMEM_LIMIT_BYTES)


def _sigmoid(x):
    return 1.0 / (1.0 + jnp.exp(-x))


def _dot(a, b):
    return jnp.dot(a, b, preferred_element_type=F32)


def _dot_t(a, b):
    return lax.dot_general(a, b, (((1,), (1,)), ((), ())), preferred_element_type=F32)


def _seg_sum(x, ones_bd):
    hi = x.astype(BF16)
    r1 = x - hi.astype(F32)
    mid = r1.astype(BF16)
    lo = (r1 - mid.astype(F32)).astype(BF16)
    return _dot(hi, ones_bd) + _dot(mid, ones_bd) + _dot(lo, ones_bd)


def _layer_norm(y, g, b):
    mu = jnp.mean(y, axis=-1, keepdims=True)
    d = y - mu
    var = jnp.mean(d * d, axis=-1, keepdims=True)
    return d * lax.rsqrt(var + LN_EPS) * g + b


def _mod_row_map(n_prompt_blocks, blocks_per_sample):
    def row(i):
        return jnp.where(i < n_prompt_blocks, 0, 1 + (i - n_prompt_blocks) // blocks_per_sample)
    return row


def _mod_kernel(c_ref, w_ref, b_ref, o_ref):
    cv = c_ref[...]
    h = (cv * _sigmoid(cv)).astype(BF16)
    o_ref[0] = _dot(h, w_ref[0].astype(BF16)) + b_ref[0]


def _modulation(cvecs, w_mod, b_mod):
    L = w_mod.shape[0]
    n = N_MOD * D_MODEL
    out = pl.pallas_call(
        _mod_kernel,
        out_shape=jax.ShapeDtypeStruct((L, MOD_ROWS, n), F32),
        grid=(L, n // MOD_TN),
        in_specs=[
            pl.BlockSpec((MOD_ROWS, D_MODEL), lambda l, j: (0, 0)),
            pl.BlockSpec((1, D_MODEL, MOD_TN), lambda l, j: (l, 0, j)),
            pl.BlockSpec((1, 1, MOD_TN), lambda l, j: (l, 0, j)),
        ],
        out_specs=pl.BlockSpec((1, MOD_ROWS, MOD_TN), lambda l, j: (l, 0, j)),
        compiler_params=_cparams("arbitrary", "arbitrary"),
        name="modulation",
    )(cvecs, w_mod, b_mod.reshape(L, 1, n))
    return out.reshape(L, MOD_ROWS, N_MOD, D_MODEL)


def _ffn_kernel(x_ref, mod_ref, win_ref, wout_ref, g_ref, b_ref, o_ref, act_ref, *, sub, alpha):
    x = x_ref[...]
    shift = mod_ref[0, 3 * sub:3 * sub + 1, :]
    scale = mod_ref[0, 3 * sub + 1:3 * sub + 2, :]
    gate = mod_ref[0, 3 * sub + 2:3 * sub + 3, :]
    h = (x * (1.0 + scale) + shift).astype(BF16)
    for c in range(D_FF // FFN_CK):
        a = _dot(h, win_ref[:, c * FFN_CK:(c + 1) * FFN_CK])
        b = _dot(h, win_ref[:, D_FF + c * FFN_CK:D_FF + (c + 1) * FFN_CK])
        act_ref[:, c * FFN_CK:(c + 1) * FFN_CK] = (a * _sigmoid(a) * b).astype(BF16)
    f = _dot(act_ref[...], wout_ref[...])
    y = alpha * x + (0.5 * gate) * f
    o_ref[...] = _layer_norm(y, g_ref[...], b_ref[...])


def _ffn_sublayer(x, mod, w_in, w_out, ln_g, ln_b, *, sub, alpha, row_map):
    n = x.shape[0]
    tm = ROW_TILE
    const = lambda i: (0, 0)
    return pl.pallas_call(
        functools.partial(_ffn_kernel, sub=sub, alpha=alpha),
        out_shape=jax.ShapeDtypeStruct((n, D_MODEL), F32),
        grid=(n // tm,),
        in_specs=[
            pl.BlockSpec((tm, D_MODEL), lambda i: (i, 0)),
            pl.BlockSpec((1, N_MOD, D_MODEL), lambda i: (row_map(i), 0, 0)),
            pl.BlockSpec((D_MODEL, 2 * D_FF), const),
            pl.BlockSpec((D_FF, D_MODEL), const),
            pl.BlockSpec((1, D_MODEL), const),
            pl.BlockSpec((1, D_MODEL), const),
        ],
        out_specs=pl.BlockSpec((tm, D_MODEL), lambda i: (i, 0)),
        scratch_shapes=[pltpu.VMEM((tm, D_FF), BF16)],
        compiler_params=_cparams("arbitrary"),
        name="ffn_sublayer",
    )(x, mod, w_in, w_out, ln_g.reshape(1, D_MODEL), ln_b.reshape(1, D_MODEL))


def _inproj_kernel(x_ref, mod_ref, w_ref, *out_refs):
    x = x_ref[...]
    shift = mod_ref[0, 3:4, :]
    scale = mod_ref[0, 4:5, :]
    h = (x * (1.0 + scale) + shift).astype(BF16)
    off = 0
    for o_ref, width in zip(out_refs, IN_SPLITS):
        o_ref[...] = _dot(h, w_ref[:, off:off + width])
        off += width


def _in_projection(x, mod, w_in, *, row_map):
    n = x.shape[0]
    tm = ROW_TILE
    return pl.pallas_call(
        _inproj_kernel,
        out_shape=tuple(jax.ShapeDtypeStruct((n, w), F32) for w in IN_SPLITS),
        grid=(n // tm,),
        in_specs=[
            pl.BlockSpec((tm, D_MODEL), lambda i: (i, 0)),
            pl.BlockSpec((1, N_MOD, D_MODEL), lambda i: (row_map(i), 0, 0)),
            pl.BlockSpec((D_MODEL, IN_WIDTH), lambda i: (0, 0)),
        ],
        out_specs=tuple(pl.BlockSpec((tm, w), lambda i: (i, 0)) for w in IN_SPLITS),
        compiler_params=_cparams("arbitrary"),
        name="in_projection",
    )(x, mod, w_in)


def _rope_tables(T):
    n_rows = T // GRID_W
    row = np.repeat(np.arange(n_rows), GRID_W).astype(np.float32)
    col = np.tile(np.arange(GRID_W), n_rows).astype(np.float32)
    inv = (1.0 / (ROPE_THETA ** (np.arange(0, ROPE_AXIS_DIM, 2, dtype=np.float32) / ROPE_AXIS_DIM))).astype(np.float32)
    ang_r = row[:, None] * inv
    ang_c = col[:, None] * inv
    z = np.zeros_like(ang_r)
    cos = np.concatenate([np.cos(ang_r), np.cos(ang_r), np.cos(ang_c), np.cos(ang_c)], axis=1)
    s_up = np.concatenate([-np.sin(ang_r), z, -np.sin(ang_c), z], axis=1)
    s_dn = np.concatenate([z, np.sin(ang_r), z, np.sin(ang_c)], axis=1)
    tile2 = lambda t: jnp.asarray(np.concatenate([t, t], axis=1), F32)
    return tile2(cos), tile2(s_up), tile2(s_dn)


def _rope128(x, cos, s_up, s_dn):
    up = pltpu.roll(x, LANES - ROPE_AXIS_DIM // 2, axis=1)
    dn = pltpu.roll(x, ROPE_AXIS_DIM // 2, axis=1)
    return x * cos + up * s_up + dn * s_dn


def _softmax_rows(s):
    m = jnp.max(s, axis=-1, keepdims=True)
    e = jnp.exp(s - m)
    return e / jnp.sum(e, axis=-1, keepdims=True)


def _attn_a_kernel(*refs, latent, T, past):
    o_ref, kn_ref, q_s, k_s, v_s = refs[-5:]
    if latent:
        aq_ref, ak_ref, av_ref, gq_ref, gk_ref, ones_ref, cos_ref, sup_ref, sdn_ref, ck_ref, cv_ref = refs[:11]
    else:
        aq_ref, ak_ref, av_ref, gq_ref, gk_ref, ones_ref = refs[:6]
    ones = ones_ref[...]
    inv_d = 1.0 / HEAD_DIM
    aq = aq_ref[...]
    qn = aq * lax.rsqrt(_seg_sum(aq * aq, ones) * inv_d + RMS_EPS) * gq_ref[...]
    ak = ak_ref[...]
    kn = ak * lax.rsqrt(_seg_sum(ak * ak, ones[:A_KV_W, :A_KV_W]) * inv_d + RMS_EPS) * gk_ref[...]
    kn_ref[...] = kn
    if latent:
        cos, sup, sdn = cos_ref[...], sup_ref[...], sdn_ref[...]
        for j in range(A_Q_W // LANES):
            q_s[:, j * LANES:(j + 1) * LANES] = _rope128(
                qn[:, j * LANES:(j + 1) * LANES], cos, sup, sdn).astype(BF16)
        kr = _rope128(kn, cos, sup, sdn).astype(BF16)
    else:
        q_s[...] = qn.astype(BF16)
        kr = kn.astype(BF16)
    av = av_ref[...].astype(BF16)
    for h in range(ATTN_KV_HEADS):
        k_s[h, 0:T, :] = kr[:, h * HEAD_DIM:(h + 1) * HEAD_DIM]
        v_s[h, 0:T, :] = av[:, h * HEAD_DIM:(h + 1) * HEAD_DIM]
        if latent:
            k_s[h, T:T + past, :] = ck_ref[0, h].astype(BF16)
            v_s[h, T:T + past, :] = cv_ref[0, h].astype(BF16)
    scale = HEAD_DIM ** -0.5
    for qb in range(T // Q_TILE):
        rows = slice(qb * Q_TILE, (qb + 1) * Q_TILE)
        outs = []
        for j in range(ATTN_HEADS):
            h = j // ATTN_GROUP
            q = q_s[rows, j * HEAD_DIM:(j + 1) * HEAD_DIM]
            s = _dot_t(q, k_s[h]) * scale
            p = _softmax_rows(s).astype(BF16)
            outs.append(_dot(p, v_s[h]))
        o_ref[rows, :] = jnp.concatenate(outs, axis=-1)


def _attn_a(aq, ak, av, gq, gk, ones_bd, *, B, T, row0, latent, rope=None, ctx_k=None, ctx_v=None, shared=None):
    blk0 = row0 // T
    n = aq.shape[0]
    past = ctx_k.shape[2] if latent else 0
    rows = lambda w: pl.BlockSpec((T, w), lambda b: (blk0 + b, 0))
    const = lambda shp: pl.BlockSpec(shp, lambda b: (0,) * len(shp))
    in_specs = [rows(A_Q_W), rows(A_KV_W), rows(A_KV_W), const((1, A_Q_W)), const((1, A_KV_W)),
                const((A_Q_W, A_Q_W))]
    args = [aq, ak, av, gq, gk, ones_bd]
    if latent:
        in_specs += [const((T, LANES))] * 3
        in_specs += [pl.BlockSpec((1, ATTN_KV_HEADS, past, HEAD_DIM), lambda b: (b, 0, 0, 0))] * 2
        args += list(rope) + [ctx_k, ctx_v]
    aliases = {}
    if shared is not None:
        aliases = {len(args): 0}
        in_specs.append(pl.BlockSpec(memory_space=pl.ANY))
        args.append(shared)
    return pl.pallas_call(
        functools.partial(_attn_a_kernel, latent=latent, T=T, past=past),
        out_shape=(jax.ShapeDtypeStruct((n, A_Q_W), F32), jax.ShapeDtypeStruct((B * T, A_KV_W), F32)),
        grid=(B,),
        in_specs=in_specs,
        out_specs=(pl.BlockSpec((T, A_Q_W), lambda b: (blk0 + b, 0)), pl.BlockSpec((T, A_KV_W), lambda b: (b, 0))),
        input_output_aliases=aliases,
        scratch_shapes=[pltpu.VMEM((T, A_Q_W), BF16),
                        pltpu.VMEM((ATTN_KV_HEADS, T + past, HEAD_DIM), BF16),
                        pltpu.VMEM((ATTN_KV_HEADS, T + past, HEAD_DIM), BF16)],
        compiler_params=_cparams("arbitrary"),
        name="attn_gqa_latent" if latent else "attn_gqa_context",
    )(*args)


def _attn_b_kernel(*refs, latent, T, past, lam_init):
    o_ref, k_s, v_s = refs[-3:]
    if latent:
        bq_ref, bk_ref, bv_ref, lv_ref, sg_ref, cos_ref, sup_ref, sdn_ref, ck_ref, cv_ref = refs[:10]
    else:
        bq_ref, bk_ref, bv_ref, lv_ref, sg_ref = refs[:5]
    lv = lv_ref[...]
    d01 = jnp.sum(lv[0:1, :] * lv[1:2, :], axis=-1, keepdims=True)
    d23 = jnp.sum(lv[2:3, :] * lv[3:4, :], axis=-1, keepdims=True)
    lam = jnp.exp(d01) - jnp.exp(d23) + lam_init
    q = bq_ref[...]
    k = bk_ref[...]
    if latent:
        cos, sup, sdn = cos_ref[...], sup_ref[...], sdn_ref[...]
        q = _rope128(q, cos, sup, sdn)
        k = _rope128(k, cos, sup, sdn)
    q = q.astype(BF16)
    k_s[0:T, :] = k.astype(BF16)
    v_s[0:T, :] = bv_ref[...].astype(BF16)
    if latent:
        k_s[T:T + past, :] = ck_ref[0, 0].astype(BF16)
        v_s[T:T + past, :] = cv_ref[0, 0].astype(BF16)
    scale = HEAD_DIM ** -0.5
    inv_d = 1.0 / (2 * HEAD_DIM)
    k1 = k_s[:, 0:HEAD_DIM]
    k2 = k_s[:, HEAD_DIM:2 * HEAD_DIM]
    v = v_s[...]
    for qb in range(T // Q_TILE):
        rows = slice(qb * Q_TILE, (qb + 1) * Q_TILE)
        s1 = _dot_t(q[rows, 0:HEAD_DIM], k1) * scale
        s2 = _dot_t(q[rows, HEAD_DIM:2 * HEAD_DIM], k2) * scale
        p = (_softmax_rows(s1) - lam * _softmax_rows(s2)).astype(BF16)
        o = _dot(p, v)
        ms = jnp.mean(o * o, axis=-1, keepdims=True)
        o_ref[rows, :] = o * lax.rsqrt(ms + RMS_EPS) * sg_ref[...] * (1.0 - lam_init)


def _attn_b(bq, bk, bv, lam_vec, subln_g, *, B, T, row0, latent, lam_init, rope=None, ctx_k=None, ctx_v=None,
            shared=None):
    blk0 = row0 // T
    n = bq.shape[0]
    past = ctx_k.shape[2] if latent else 0
    dh = 2 * HEAD_DIM
    rows = pl.BlockSpec((T, dh), lambda b, h: (blk0 + b, h))
    const = lambda shp: pl.BlockSpec(shp, lambda b, h: (0,) * len(shp))
    in_specs = [rows, rows, rows, const((4, HEAD_DIM)), const((1, dh))]
    args = [bq, bk, bv, lam_vec, subln_g]
    if latent:
        in_specs += [const((T, LANES))] * 3
        in_specs += [pl.BlockSpec((1, 1, past, dh), lambda b, h: (b, h, 0, 0))] * 2
        args += list(rope) + [ctx_k, ctx_v]
    aliases = {}
    if shared is not None:
        aliases = {len(args): 0}
        in_specs.append(pl.BlockSpec(memory_space=pl.ANY))
        args.append(shared)
    return pl.pallas_call(
        functools.partial(_attn_b_kernel, latent=latent, T=T, past=past, lam_init=lam_init),
        out_shape=jax.ShapeDtypeStruct((n, B_V_W), F32),
        grid=(B, DIFF_HEADS),
        in_specs=in_specs,
        out_specs=pl.BlockSpec((T, dh), lambda b, h: (blk0 + b, h)),
        input_output_aliases=aliases,
        scratch_shapes=[pltpu.VMEM((T + past, dh), BF16), pltpu.VMEM((T + past, dh), BF16)],
        compiler_params=_cparams("arbitrary", "arbitrary"),
        name="attn_diff_latent" if latent else "attn_diff_context",
    )(*args)


def _rwkv_prep_kernel(cin_ref, prev_ref, next_ref, mu_ref, w0_ref, w2_ref, a0_ref, a2_ref, g2_ref,
                      ka_ref, rk_ref, ones_ref, *rest, tm, per_seq):
    r_o, k_o, v_o, w_o, a_o, g_o, bonus_o = rest[-7:]
    pos = pl.program_id(0) % per_seq
    row_before = jnp.where(pos == 0, 0.0, prev_ref[SUBLANES - 1:SUBLANES, :])
    row_after = jnp.where(pos == per_seq - 1, 0.0, next_ref[0:1, :])
    x = cin_ref[...]
    t_idx = lax.broadcasted_iota(jnp.int32, (tm, 1), 0)
    prev = jnp.where(t_idx == 0, row_before, pltpu.roll(x, 1, axis=0))
    nxt = jnp.where(t_idx == tm - 1, row_after, pltpu.roll(x, tm - 1, axis=0))
    x = x + mu_ref[...] * (0.5 * (prev + nxt) - x)
    bw = BRANCH_WIDTH
    r = x[:, 0:bw]
    k = x[:, bw:2 * bw]
    v = x[:, 2 * bw:3 * bw]
    off = 3 * bw
    wl = x[:, off:off + 2 * W_LORA]
    al = x[:, off + 2 * W_LORA:off + 2 * W_LORA + 2 * A_LORA]
    gl = x[:, off + 2 * W_LORA + 2 * A_LORA:]
    ones = ones_ref[...]
    r_o[...] = r
    k_o[...] = k
    v_o[...] = v
    g_o[...] = _dot(_sigmoid(gl).astype(BF16), g2_ref[...])
    tw = jnp.tanh(wl).astype(BF16)
    alb = al.astype(BF16)
    decay_rate = math.exp(-0.5)
    bonus = jnp.zeros_like(r)
    for d in range(2):
        w_logit = w0_ref[d:d + 1, :] + _dot(tw[:, d * W_LORA:(d + 1) * W_LORA], w2_ref[d])
        w_o[d] = jnp.exp(-decay_rate * _sigmoid(w_logit))
        a = _sigmoid(a0_ref[d:d + 1, :] + _dot(alb[:, d * A_LORA:(d + 1) * A_LORA], a2_ref[d]))
        a_o[d] = a
        kc = k * (1.0 + (a - 1.0) * ka_ref[...])
        bonus = bonus + _seg_sum(r * kc * rk_ref[...], ones) * v
    bonus_o[...] = bonus


def _rwkv_prep(cin, p, ones_bd, *, row0, n_rows, t_seq, shared=None):
    n = cin.shape[0]
    tm = ROW_TILE
    bw = BRANCH_WIDTH
    blk0 = row0 // tm
    halo = tm // SUBLANES
    last = n // SUBLANES - 1
    const = lambda shp: pl.BlockSpec(shp, lambda i: (0,) * len(shp))
    tok = jax.ShapeDtypeStruct((n_rows, bw), F32)
    tok2 = jax.ShapeDtypeStruct((2, n_rows, bw), F32)
    full = jax.ShapeDtypeStruct((n, bw), F32)
    tspec = pl.BlockSpec((tm, bw), lambda i: (i, 0))
    tspec2 = pl.BlockSpec((2, tm, bw), lambda i: (0, i, 0))
    fspec = pl.BlockSpec((tm, bw), lambda i: (blk0 + i, 0))
    in_specs = [pl.BlockSpec((tm, C_IN_W), lambda i: (blk0 + i, 0)),
                pl.BlockSpec((SUBLANES, C_IN_W), lambda i: (jnp.maximum((blk0 + i) * halo - 1, 0), 0)),
                pl.BlockSpec((SUBLANES, C_IN_W), lambda i: (jnp.minimum((blk0 + i + 1) * halo, last), 0)),
                const((1, C_IN_W)), const((2, bw)), const((2, W_LORA, bw)), const((2, bw)),
                const((2, A_LORA, bw)), const((G_LORA, bw)), const((1, bw)), const((1, bw)),
                const((bw, bw))]
    args = [cin, cin, cin, p['mu'], p['w0'], p['w2'], p['a0'], p['a2'], p['g2'], p['k_a'], p['r_k'], ones_bd]
    aliases = {}
    if shared is not None:
        aliases = {len(args): 5, len(args) + 1: 6}
        in_specs += [pl.BlockSpec(memory_space=pl.ANY)] * 2
        args += list(shared)
    return pl.pallas_call(
        functools.partial(_rwkv_prep_kernel, tm=tm, per_seq=t_seq // tm),
        out_shape=(tok, tok, tok, tok2, tok2, full, full),
        grid=(n_rows // tm,),
        in_specs=in_specs,
        out_specs=(tspec, tspec, tspec, tspec2, tspec2, fspec, fspec),
        input_output_aliases=aliases,
        compiler_params=_cparams("arbitrary"),
        name="rwkv_streams",
    )(*args)


def _scan_kernel(*refs, tc, nv, dir_lanes):
    if dir_lanes:
        (rf, kf, vf, wf, af, rb, kb, vb, wb, ab, bwd_ref, kk_ref, ka_ref, s0_ref,
         of_ref, ob_ref, sT_ref, S, al_s, wr_s, be_s, kc_s, w_s, v_s, c_s) = refs
        bwd = bwd_ref[...] > 0.0
    else:
        (rf, kf, vf, wf, af, kk_ref, ka_ref, s0_ref, of_ref, sT_ref,
         S, al_s, wr_s, be_s, kc_s, w_s, v_s, c_s) = refs
    d = pl.program_id(0)
    j = pl.program_id(2)
    n = RWKV_HEAD_DIM

    @pl.when(j == 0)
    def _():
        S[...] = s0_ref[...]

    def derive(i, carry):
        if dir_lanes:
            ib = tc - 1 - i
            r = jnp.where(bwd, rb[ib], rf[i])
            k = jnp.where(bwd, kb[ib], kf[i])
            v = jnp.where(bwd, vb[ib], vf[i])
            w = jnp.where(bwd, wb[ib], wf[i])
            a = jnp.where(bwd, ab[ib], af[i])
        else:
            row = i + d * (tc - 1 - 2 * i)
            r, k, v, w, a = rf[row], kf[row], vf[row], wf[row], af[row]
        kkv = k * kk_ref[...]
        kk = kkv * lax.rsqrt(jnp.sum(kkv * kkv, axis=0, keepdims=True) + 1e-12)
        kc = k * (1.0 + (a - 1.0) * ka_ref[...])
        beta = kk * a
        al_s[i] = -kk
        wr_s[i] = w * r
        be_s[i] = beta
        kc_s[i] = kc
        w_s[i] = w
        v_s[i] = v
        c_s[i, 0:1, :] = jnp.sum(beta * r, axis=0, keepdims=True)
        c_s[i, 1:2, :] = jnp.sum(kc * r, axis=0, keepdims=True)
        return carry

    lax.fori_loop(0, tc, derive, 0)

    def step(i, carry):
        def body_a(kx, acc):
            sa, y = acc
            sk = S[kx]
            return sa + sk * al_s[i, pl.ds(kx, 1), :], y + sk * wr_s[i, pl.ds(kx, 1), :]

        zero = jnp.zeros((nv, LANES), F32)
        sa, y = lax.fori_loop(0, n, body_a, (zero, zero), unroll=SCAN_UNROLL)
        v = v_s[i]

        def body_b(kx, c):
            S[kx] = (S[kx] * w_s[i, pl.ds(kx, 1), :] + be_s[i, pl.ds(kx, 1), :] * sa
                     + kc_s[i, pl.ds(kx, 1), :] * v)
            return c

        lax.fori_loop(0, n, body_b, 0, unroll=SCAN_UNROLL)
        out = y + sa * c_s[i, 0:1, :] + v * c_s[i, 1:2, :]
        if dir_lanes:
            of_ref[i] = out
            ob_ref[tc - 1 - i] = out
        else:
            of_ref[i + d * (tc - 1 - 2 * i)] = out
        return carry

    lax.fori_loop(0, tc, step, 0)

    @pl.when(j == pl.num_programs(2) - 1)
    def _():
        sT_ref[...] = S[...]


def _scan_scratch(nv):
    n = RWKV_HEAD_DIM
    tc = SCAN_TC
    return ([pltpu.VMEM((n, nv, LANES), F32)] + [pltpu.VMEM((tc, n, LANES), F32)] * 5
            + [pltpu.VMEM((tc, nv, LANES), F32), pltpu.VMEM((tc, SUBLANES, LANES), F32)])


def _rwkv_scan_groups(r, k, v, w, a, kk_t, ka_t, s0):
    LG, T = r.shape[:2]
    n = RWKV_HEAD_DIM
    tc = SCAN_TC
    nt = T // tc
    tb = lambda d, j: j + d * (nt - 1 - 2 * j)
    shared = pl.BlockSpec((None, tc, n, LANES), lambda d, g, j: (g, tb(d, j), 0, 0))
    perdir = pl.BlockSpec((None, None, tc, n, LANES), lambda d, g, j: (d, g, tb(d, j), 0, 0))
    par = pl.BlockSpec((n, LANES), lambda d, g, j: (0, 0))
    st = pl.BlockSpec((None, None, n, n, LANES), lambda d, g, j: (d, g, 0, 0, 0))
    return pl.pallas_call(
        functools.partial(_scan_kernel, tc=tc, nv=n, dir_lanes=False),
        out_shape=(jax.ShapeDtypeStruct((2, LG, T, n, LANES), F32), jax.ShapeDtypeStruct((2, LG, n, n, LANES), F32)),
        grid=(2, LG, nt),
        in_specs=[shared, shared, shared, perdir, perdir, par, par, st],
        out_specs=(perdir, st),
        scratch_shapes=_scan_scratch(n),
        compiler_params=_cparams("arbitrary", "arbitrary", "arbitrary"),
        name="rwkv_scan_context",
    )(r, k, v, w, a, kk_t, ka_t, s0)


def _rwkv_scan_lanes(r, k, v, w, a, bwd, kk_t, ka_t, s0):
    T = r.shape[0]
    nv = v.shape[1]
    n = RWKV_HEAD_DIM
    tc = SCAN_TC
    nt = T // tc
    fwd_k = pl.BlockSpec((tc, n, LANES), lambda d, g, j: (j, 0, 0))
    bwd_k = pl.BlockSpec((tc, n, LANES), lambda d, g, j: (nt - 1 - j, 0, 0))
    fwd_v = pl.BlockSpec((tc, nv, LANES), lambda d, g, j: (j, 0, 0))
    bwd_v = pl.BlockSpec((tc, nv, LANES), lambda d, g, j: (nt - 1 - j, 0, 0))
    par = pl.BlockSpec((n, LANES), lambda d, g, j: (0, 0))
    st = pl.BlockSpec((n, nv, LANES), lambda d, g, j: (0, 0, 0))
    o_sds = jax.ShapeDtypeStruct((T, nv, LANES), F32)
    return pl.pallas_call(
        functools.partial(_scan_kernel, tc=tc, nv=nv, dir_lanes=True),
        out_shape=(o_sds, o_sds, jax.ShapeDtypeStruct((n, nv, LANES), F32)),
        grid=(1, 1, nt),
        in_specs=[fwd_k, fwd_k, fwd_v, fwd_k, fwd_k, bwd_k, bwd_k, bwd_v, bwd_k, bwd_k,
                  pl.BlockSpec((1, LANES), lambda d, g, j: (0, 0)), par, par, st],
        out_specs=(fwd_v, bwd_v, st),
        scratch_shapes=_scan_scratch(nv),
        compiler_params=_cparams("arbitrary", "arbitrary", "arbitrary"),
        name="rwkv_scan_latent",
    )(r, k, v, w, a, r, k, v, w, a, bwd, kk_t, ka_t, s0)


def _to_lanes(x, B, T):
    lead = x.shape[:-2]
    x = x.reshape(lead + (B, T, RWKV_HEADS, RWKV_HEAD_DIM))
    nl = len(lead)
    perm = tuple(range(nl)) + (nl + 1, nl + 3, nl + 0, nl + 2)
    return x.transpose(perm).reshape(lead + (T, RWKV_HEAD_DIM, B * RWKV_HEADS))


def _lane_param(p, vl):
    t = jnp.repeat(p.reshape(RWKV_HEADS, RWKV_HEAD_DIM).T, vl, axis=1)
    return jnp.tile(t, (1, LANES // (RWKV_HEADS * vl)))


def _rwkv_context(r, k, v, w, a, k_k, k_a, B, T):
    n = RWKV_HEAD_DIM
    nb = B * RWKV_HEADS
    LG = -(-nb // LANES)

    def groups(x):
        f = _to_lanes(x, B, T)
        f = jnp.pad(f, [(0, 0)] * (f.ndim - 1) + [(0, LG * LANES - nb)])
        f = f.reshape(f.shape[:-1] + (LG, LANES))
        return jnp.moveaxis(f, -2, -4)

    s0 = jnp.zeros((2, LG, n, n, LANES), F32)
    o, sT = _rwkv_scan_groups(groups(r), groups(k), groups(v), groups(w), groups(a),
                              _lane_param(k_k, 1), _lane_param(k_a, 1), s0)
    o = jnp.moveaxis(o, 1, 3).reshape(2, T, n, LG * LANES)[..., :nb]
    o = o.reshape(2, T, n, B, RWKV_HEADS).transpose(0, 3, 1, 4, 2).reshape(2, B * T, BRANCH_WIDTH)
    s = jnp.moveaxis(sT, 1, 3).reshape(2, n, n, LG * LANES)[..., :nb]
    s = s.reshape(2, n, n, B, RWKV_HEADS).transpose(3, 0, 4, 2, 1)
    return o[0], o[1], s


def _rwkv_latent(r, k, v, w, a, k_k, k_a, state0, B, T):
    n = RWKV_HEAD_DIM
    H = RWKV_HEADS
    vl = min(SUBLANES, LANES // (2 * B * H))
    assert vl >= 1 and vl & (vl - 1) == 0
    nv = n // vl
    half = B * H * vl
    pad = lambda x: jnp.pad(x, [(0, 0)] * (x.ndim - 1) + [(0, LANES - 2 * half)])

    def k_lanes(x):
        return jnp.repeat(_to_lanes(x, B, T), vl, axis=-1)

    def shared_k(x):
        f = k_lanes(x)
        return pad(jnp.concatenate([f, f], axis=-1))

    def per_dir(x):
        f = k_lanes(x)
        return pad(jnp.concatenate([f[0], f[1]], axis=-1))

    vv = v.reshape(B, T, H, vl, nv).transpose(1, 4, 0, 2, 3).reshape(T, nv, half)
    vv = pad(jnp.concatenate([vv, vv], axis=-1))
    s0 = state0.reshape(B, 2, H, vl, nv, n).transpose(5, 4, 1, 0, 2, 3).reshape(n, nv, 2 * half)
    bwd = (jnp.arange(LANES) >= half).astype(F32)[None]
    o_f, o_b, _ = _rwkv_scan_lanes(shared_k(r), shared_k(k), vv, per_dir(w), per_dir(a), bwd,
                                   _lane_param(k_k, vl), _lane_param(k_a, vl), pad(s0))
    tok = lambda o: o.reshape(T, nv, B, H, vl).transpose(2, 0, 3, 4, 1).reshape(B * T, BRANCH_WIDTH)
    return tok(o_f[..., :half]), tok(o_b[..., half:2 * half])


def _merge_kernel(x_ref, mod_ref, gates_ref, oa_ref, ob_ref, sfp_ref, sbp_ref, sfs_ref, sbs_ref, g_ref, bonus_ref,
                  gng_ref, gnb_ref, ones_ref, wup_ref, wout_ref, lng_ref, lnb_ref, o_ref,
                  *, alpha, n_prompt_blocks):
    ones = ones_ref[...]
    inv_n = 1.0 / RWKV_HEAD_DIM
    in_prompt = pl.program_id(0) < n_prompt_blocks
    o = jnp.where(in_prompt, sfp_ref[...] + sbp_ref[...], sfs_ref[...] + sbs_ref[...])
    mu = _seg_sum(o, ones) * inv_n
    d = o - mu
    var = _seg_sum(d * d, ones) * inv_n
    oc = d * lax.rsqrt(var + RWKV_GN_EPS) * gng_ref[...] + gnb_ref[...]
    oc = (oc + bonus_ref[...]) * g_ref[...]
    branches = (oa_ref[...], ob_ref[...], oc)
    mix = None
    for i in range(N_BRANCH):
        up = _dot(branches[i].astype(BF16), wup_ref[i])
        term = _sigmoid(gates_ref[:, i * D_MODEL:(i + 1) * D_MODEL]) * up
        mix = term if mix is None else mix + term
    h = _dot(mix.astype(BF16), wout_ref[...])
    x = x_ref[...]
    y = alpha * x + mod_ref[0, 5:6, :] * h
    o_ref[...] = _layer_norm(y, lng_ref[...], lnb_ref[...])


def _merge_sublayer(x, mod, gates, oa, ob, sf_p, sb_p, sf_s, sb_s, g, bonus, gn_g, gn_b, ones_bd, w_up, w_out,
                    ln_g, ln_b, *, alpha, row_map):
    n = x.shape[0]
    tm = ROW_TILE
    bw = BRANCH_WIDTH
    npb = sf_p.shape[0] // tm
    rows = lambda w: pl.BlockSpec((tm, w), lambda i: (i, 0))
    prow = pl.BlockSpec((tm, bw), lambda i: (jnp.minimum(i, npb - 1), 0))
    srow = pl.BlockSpec((tm, bw), lambda i: (jnp.maximum(i - npb, 0), 0))
    const = lambda shp: pl.BlockSpec(shp, lambda i: (0,) * len(shp))
    return pl.pallas_call(
        functools.partial(_merge_kernel, alpha=alpha, n_prompt_blocks=npb),
        out_shape=jax.ShapeDtypeStruct((n, D_MODEL), F32),
        grid=(n // tm,),
        in_specs=[rows(D_MODEL), pl.BlockSpec((1, N_MOD, D_MODEL), lambda i: (row_map(i), 0, 0)),
                  rows(N_BRANCH * D_MODEL), rows(bw), rows(bw), prow, prow, srow, srow, rows(bw), rows(bw),
                  const((1, bw)), const((1, bw)), const((bw, bw)),
                  const((N_BRANCH, bw, D_MODEL)), const((D_MODEL, D_MODEL)),
                  const((1, D_MODEL)), const((1, D_MODEL))],
        out_specs=rows(D_MODEL),
        compiler_params=_cparams("arbitrary"),
        name="merge_sublayer",
    )(x, mod, gates, oa, ob, sf_p, sb_p, sf_s, sb_s, g, bonus, gn_g, gn_b, ones_bd, w_up, w_out,
      ln_g.reshape(1, D_MODEL), ln_b.reshape(1, D_MODEL))


def _heads_first(x, B, T, H):
    return x.reshape(B, T, H, x.shape[-1] // H).transpose(0, 2, 1, 3)


def kernel(x_prompt, x_sample, cache_attn_k, cache_attn_v, cache_diff_k, cache_diff_v, state_rwkv, c, c_ctx,
           w_mod, b_mod, ln_g, ln_b, ffn_w_in, ffn_w_out, w_in, qk_norm_g, diff_lambda, diff_subln_g,
           rwkv_mu, rwkv_w0, rwkv_w2, rwkv_a0, rwkv_a2, rwkv_g2, rwkv_k_k, rwkv_k_a, rwkv_r_k,
           rwkv_gn_g, rwkv_gn_b, w_up, w_out):
    Bp, Tp, _ = x_prompt.shape
    Bs, Ts, _ = x_sample.shape
    depth = w_mod.shape[0]
    alpha = (2.0 * depth) ** 0.25
    n_p, n_s = Bp * Tp, Bs * Ts
    assert n_p % ROW_TILE == 0 and Ts % ROW_TILE == 0 and 1 + Bs <= MOD_ROWS
    row_map = _mod_row_map(n_p // ROW_TILE, Ts // ROW_TILE)

    x = jnp.concatenate([x_prompt.reshape(n_p, D_MODEL), x_sample.reshape(n_s, D_MODEL)], axis=0)
    cvecs = jnp.concatenate([c_ctx[None], c, jnp.zeros((MOD_ROWS - 1 - Bs, D_MODEL), F32)], axis=0)
    mod_all = _modulation(cvecs, w_mod, b_mod)

    seg = np.arange(BRANCH_WIDTH) // HEAD_DIM
    ones_bd = jnp.asarray(seg[:, None] == seg[None, :], BF16)
    rope = _rope_tables(Ts)
    bw = BRANCH_WIDTH

    n_ak, n_av, n_dk, n_dv, n_st = [], [], [], [], []
    for l in range(depth):
        lam_init = 0.8 - 0.6 * math.exp(-0.3 * l)
        mod = mod_all[l]
        x = _ffn_sublayer(x, mod, ffn_w_in[l, 0].astype(BF16), ffn_w_out[l, 0].astype(BF16),
                          ln_g[l, 0], ln_b[l, 0], sub=0, alpha=alpha, row_map=row_map)
        gates, aq, ak, av, bq, bk, bv, cin = _in_projection(x, mod, w_in[l].astype(BF16), row_map=row_map)

        gq = jnp.tile(qk_norm_g[l, 0], ATTN_HEADS)[None]
        gk = jnp.tile(qk_norm_g[l, 1], ATTN_KV_HEADS)[None]
        oa_p, kn_p = _attn_a(aq, ak, av, gq, gk, ones_bd, B=Bp, T=Tp, row0=0, latent=False)
        oa, _ = _attn_a(aq, ak, av, gq, gk, ones_bd, B=Bs, T=Ts, row0=n_p, latent=True, rope=rope,
                        ctx_k=cache_attn_k[:, l], ctx_v=cache_attn_v[:, l], shared=oa_p)
        sg = diff_subln_g[l][None]
        ob_p = _attn_b(bq, bk, bv, diff_lambda[l], sg, B=Bp, T=Tp, row0=0, latent=False, lam_init=lam_init)
        ob = _attn_b(bq, bk, bv, diff_lambda[l], sg, B=Bs, T=Ts, row0=n_p, latent=True, lam_init=lam_init,
                     rope=rope, ctx_k=cache_diff_k[:, l], ctx_v=cache_diff_v[:, l], shared=ob_p)

        rp = {'mu': rwkv_mu[l][None], 'w0': rwkv_w0[l], 'w2': rwkv_w2[l].astype(BF16), 'a0': rwkv_a0[l],
              'a2': rwkv_a2[l].astype(BF16), 'g2': rwkv_g2[l].astype(BF16),
              'k_a': rwkv_k_a[l][None], 'r_k': rwkv_r_k[l].reshape(1, bw)}
        r, k, v, w, a, gg, bonus = _rwkv_prep(cin, rp, ones_bd, row0=0, n_rows=n_p, t_seq=Tp)
        sf_p, sb_p, st = _rwkv_context(r, k, v, w, a, rwkv_k_k[l], rwkv_k_a[l], Bp, Tp)
        r, k, v, w, a, gg, bonus = _rwkv_prep(cin, rp, ones_bd, row0=n_p, n_rows=n_s, t_seq=Ts,
                                              shared=(gg, bonus))
        sf_s, sb_s = _rwkv_latent(r, k, v, w, a, rwkv_k_k[l], rwkv_k_a[l], state_rwkv[:, l], Bs, Ts)
        n_st.append(st)
        x = _merge_sublayer(x, mod, gates, oa, ob, sf_p, sb_p, sf_s, sb_s, gg, bonus,
                            rwkv_gn_g[l][None], rwkv_gn_b[l][None], ones_bd,
                            w_up[l].astype(BF16), w_out[l].astype(BF16), ln_g[l, 1], ln_b[l, 1],
                            alpha=alpha, row_map=row_map)
        x = _ffn_sublayer(x, mod, ffn_w_in[l, 1].astype(BF16), ffn_w_out[l, 1].astype(BF16),
                          ln_g[l, 2], ln_b[l, 2], sub=2, alpha=alpha, row_map=row_map)

        n_ak.append(_heads_first(kn_p, Bp, Tp, ATTN_KV_HEADS))
        n_av.append(_heads_first(av[:n_p], Bp, Tp, ATTN_KV_HEADS))
        n_dk.append(_heads_first(bk[:n_p], Bp, Tp, DIFF_HEADS))
        n_dv.append(_heads_first(bv[:n_p], Bp, Tp, DIFF_HEADS))

    y_prompt = x[:n_p].reshape(Bp, Tp, D_MODEL)
    y_sample = x[n_p:].reshape(Bs, Ts, D_MODEL)
    stack = lambda parts: jnp.stack(parts, axis=1)
    return (y_prompt, y_sample, stack(n_ak), stack(n_av), stack(n_dk), stack(n_dv), stack(n_st))
```

```python
import functools
import math

import numpy as np
import jax
import jax.numpy as jnp
from jax import lax
from jax.experimental import pallas as pl
from jax.experimental.pallas import tpu as pltpu

F32 = jnp.float32
BF16 = jnp.bfloat16

D_MODEL = 1024
GRID_W = 64
HEAD_DIM = 64
ATTN_HEADS = 8
ATTN_KV_HEADS = 2
ATTN_GROUP = ATTN_HEADS // ATTN_KV_HEADS
DIFF_HEADS = 4
RWKV_HEADS = 8
RWKV_HEAD_DIM = 64
BRANCH_WIDTH = 512
W_LORA = 64
A_LORA = 64
G_LORA = 128
D_FF = 2816
N_BRANCH = 3
N_MOD = 9
ROPE_THETA = 10000.0
ROPE_AXIS_DIM = HEAD_DIM // 2
LN_EPS = 1e-5
RMS_EPS = 1e-6
RWKV_GN_EPS = 64e-5
A_Q_W = ATTN_HEADS * HEAD_DIM
A_KV_W = ATTN_KV_HEADS * HEAD_DIM
B_QK_W = DIFF_HEADS * 2 * HEAD_DIM
B_V_W = DIFF_HEADS * 2 * HEAD_DIM
C_IN_W = 3 * BRANCH_WIDTH + 2 * W_LORA + 2 * A_LORA + G_LORA
IN_SPLITS = (N_BRANCH * D_MODEL, A_Q_W, A_KV_W, A_KV_W, B_QK_W, B_QK_W, B_V_W, C_IN_W)
IN_WIDTH = sum(IN_SPLITS)

LANES = 128
SUBLANES = 8
VMEM_LIMIT_BYTES = 56 * 1024 * 1024

MOD_ROWS = 8
MOD_TN = 1152
FFN_CK = 256
ROW_TILE = 256
Q_TILE = 256
SCAN_TC = 32
SCAN_UNROLL = 8


def _cparams(*sem):
    return pltpu.CompilerParams(dimension_semantics=sem, vmem_limit_bytes=VMEM_LIMIT_BYTES)


def _sigmoid(x):
    return 1.0 / (1.0 + jnp.exp(-x))


def _dot(a, b):
    return jnp.dot(a, b, preferred_element_type=F32)


def _dot_t(a, b):
    return lax.dot_general(a, b, (((1,), (1,)), ((), ())), preferred_element_type=F32)


def _seg_sum(x, ones_bd):
    hi = x.astype(BF16)
    r1 = x - hi.astype(F32)
    mid = r1.astype(BF16)
    lo = (r1 - mid.astype(F32)).astype(BF16)
    return _dot(hi, ones_bd) + _dot(mid, ones_bd) + _dot(lo, ones_bd)


def _layer_norm(y, g, b):
    mu = jnp.mean(y, axis=-1, keepdims=True)
    d = y - mu
    var = jnp.mean(d * d, axis=-1, keepdims=True)
    return d * lax.rsqrt(var + LN_EPS) * g + b


def _mod_row_map(n_prompt_blocks, blocks_per_sample):
    def row(i):
        return jnp.where(i < n_prompt_blocks, 0, 1 + (i - n_prompt_blocks) // blocks_per_sample)
    return row


def _mod_kernel(c_ref, w_ref, b_ref, o_ref):
    cv = c_ref[...]
    h = (cv * _sigmoid(cv)).astype(BF16)
    o_ref[0] = _dot(h, w_ref[0].astype(BF16)) + b_ref[0]


def _modulation(cvecs, w_mod, b_mod):
    L = w_mod.shape[0]
    n = N_MOD * D_MODEL
    out = pl.pallas_call(
        _mod_kernel,
        out_shape=jax.ShapeDtypeStruct((L, MOD_ROWS, n), F32),
        grid=(L, n // MOD_TN),
        in_specs=[
            pl.BlockSpec((MOD_ROWS, D_MODEL), lambda l, j: (0, 0)),
            pl.BlockSpec((1, D_MODEL, MOD_TN), lambda l, j: (l, 0, j)),
            pl.BlockSpec((1, 1, MOD_TN), lambda l, j: (l, 0, j)),
        ],
        out_specs=pl.BlockSpec((1, MOD_ROWS, MOD_TN), lambda l, j: (l, 0, j)),
        compiler_params=_cparams("arbitrary", "arbitrary"),
        name="modulation",
    )(cvecs, w_mod, b_mod.reshape(L, 1, n))
    return out.reshape(L, MOD_ROWS, N_MOD, D_MODEL)


def _ffn_kernel(x_ref, mod_ref, win_ref, wout_ref, g_ref, b_ref, o_ref, act_ref, *, sub, alpha):
    x = x_ref[...]
    shift = mod_ref[0, 3 * sub:3 * sub + 1, :]
    scale = mod_ref[0, 3 * sub + 1:3 * sub + 2, :]
    gate = mod_ref[0, 3 * sub + 2:3 * sub + 3, :]
    h = (x * (1.0 + scale) + shift).astype(BF16)
    for c in range(D_FF // FFN_CK):
        a = _dot(h, win_ref[:, c * FFN_CK:(c + 1) * FFN_CK])
        b = _dot(h, win_ref[:, D_FF + c * FFN_CK:D_FF + (c + 1) * FFN_CK])
        act_ref[:, c * FFN_CK:(c + 1) * FFN_CK] = (a * _sigmoid(a) * b).astype(BF16)
    f = _dot(act_ref[...], wout_ref[...])
    y = alpha * x + (0.5 * gate) * f
    o_ref[...] = _layer_norm(y, g_ref[...], b_ref[...])


def _ffn_sublayer(x, mod, w_in, w_out, ln_g, ln_b, *, sub, alpha, row_map):
    n = x.shape[0]
    tm = ROW_TILE
    const = lambda i: (0, 0)
    return pl.pallas_call(
        functools.partial(_ffn_kernel, sub=sub, alpha=alpha),
        out_shape=jax.ShapeDtypeStruct((n, D_MODEL), F32),
        grid=(n // tm,),
        in_specs=[
            pl.BlockSpec((tm, D_MODEL), lambda i: (i, 0)),
            pl.BlockSpec((1, N_MOD, D_MODEL), lambda i: (row_map(i), 0, 0)),
            pl.BlockSpec((D_MODEL, 2 * D_FF), const),
            pl.BlockSpec((D_FF, D_MODEL), const),
            pl.BlockSpec((1, D_MODEL), const),
            pl.BlockSpec((1, D_MODEL), const),
        ],
        out_specs=pl.BlockSpec((tm, D_MODEL), lambda i: (i, 0)),
        scratch_shapes=[pltpu.VMEM((tm, D_FF), BF16)],
        compiler_params=_cparams("arbitrary"),
        name="ffn_sublayer",
    )(x, mod, w_in, w_out, ln_g.reshape(1, D_MODEL), ln_b.reshape(1, D_MODEL))


def _inproj_kernel(x_ref, mod_ref, w_ref, *out_refs):
    x = x_ref[...]
    shift = mod_ref[0, 3:4, :]
    scale = mod_ref[0, 4:5, :]
    h = (x * (1.0 + scale) + shift).astype(BF16)
    off = 0
    for o_ref, width in zip(out_refs, IN_SPLITS):
        o_ref[...] = _dot(h, w_ref[:, off:off + width])
        off += width


def _in_projection(x, mod, w_in, *, row_map):
    n = x.shape[0]
    tm = ROW_TILE
    return pl.pallas_call(
        _inproj_kernel,
        out_shape=tuple(jax.ShapeDtypeStruct((n, w), F32) for w in IN_SPLITS),
        grid=(n // tm,),
        in_specs=[
            pl.BlockSpec((tm, D_MODEL), lambda i: (i, 0)),
            pl.BlockSpec((1, N_MOD, D_MODEL), lambda i: (row_map(i), 0, 0)),
            pl.BlockSpec((D_MODEL, IN_WIDTH), lambda i: (0, 0)),
        ],
        out_specs=tuple(pl.BlockSpec((tm, w), lambda i: (i, 0)) for w in IN_SPLITS),
        compiler_params=_cparams("arbitrary"),
        name="in_projection",
    )(x, mod, w_in)


def _rope_tables(T):
    n_rows = T // GRID_W
    row = np.repeat(np.arange(n_rows), GRID_W).astype(np.float32)
    col = np.tile(np.arange(GRID_W), n_rows).astype(np.float32)
    inv = (1.0 / (ROPE_THETA ** (np.arange(0, ROPE_AXIS_DIM, 2, dtype=np.float32) / ROPE_AXIS_DIM))).astype(np.float32)
    ang_r = row[:, None] * inv
    ang_c = col[:, None] * inv
    z = np.zeros_like(ang_r)
    cos = np.concatenate([np.cos(ang_r), np.cos(ang_r), np.cos(ang_c), np.cos(ang_c)], axis=1)
    s_up = np.concatenate([-np.sin(ang_r), z, -np.sin(ang_c), z], axis=1)
    s_dn = np.concatenate([z, np.sin(ang_r), z, np.sin(ang_c)], axis=1)
    tile2 = lambda t: jnp.asarray(np.concatenate([t, t], axis=1), F32)
    return tile2(cos), tile2(s_up), tile2(s_dn)


def _rope128(x, cos, s_up, s_dn):
    up = pltpu.roll(x, LANES - ROPE_AXIS_DIM // 2, axis=1)
    dn = pltpu.roll(x, ROPE_AXIS_DIM // 2, axis=1)
    return x * cos + up * s_up + dn * s_dn


def _softmax_rows(s):
    m = jnp.max(s, axis=-1, keepdims=True)
    e = jnp.exp(s - m)
    return e / jnp.sum(e, axis=-1, keepdims=True)


def _attn_a_kernel(*refs, latent, T, past):
    o_ref, kn_ref, q_s, k_s, v_s = refs[-5:]
    if latent:
        aq_ref, ak_ref, av_ref, gq_ref, gk_ref, ones_ref, cos_ref, sup_ref, sdn_ref, ck_ref, cv_ref = refs[:11]
    else:
        aq_ref, ak_ref, av_ref, gq_ref, gk_ref, ones_ref = refs[:6]
    ones = ones_ref[...]
    inv_d = 1.0 / HEAD_DIM
    aq = aq_ref[...]
    qn = aq * lax.rsqrt(_seg_sum(aq * aq, ones) * inv_d + RMS_EPS) * gq_ref[...]
    ak = ak_ref[...]
    kn = ak * lax.rsqrt(_seg_sum(ak * ak, ones[:A_KV_W, :A_KV_W]) * inv_d + RMS_EPS) * gk_ref[...]
    kn_ref[...] = kn
    if latent:
        cos, sup, sdn = cos_ref[...], sup_ref[...], sdn_ref[...]
        for j in range(A_Q_W // LANES):
            q_s[:, j * LANES:(j + 1) * LANES] = _rope128(
                qn[:, j * LANES:(j + 1) * LANES], cos, sup, sdn).astype(BF16)
        kr = _rope128(kn, cos, sup, sdn).astype(BF16)
    else:
        q_s[...] = qn.astype(BF16)
        kr = kn.astype(BF16)
    av = av_ref[...].astype(BF16)
    for h in range(ATTN_KV_HEADS):
        k_s[h, 0:T, :] = kr[:, h * HEAD_DIM:(h + 1) * HEAD_DIM]
        v_s[h, 0:T, :] = av[:, h * HEAD_DIM:(h + 1) * HEAD_DIM]
        if latent:
            k_s[h, T:T + past, :] = ck_ref[0, h].astype(BF16)
            v_s[h, T:T + past, :] = cv_ref[0, h].astype(BF16)
    scale = HEAD_DIM ** -0.5
    for qb in range(T // Q_TILE):
        rows = slice(qb * Q_TILE, (qb + 1) * Q_TILE)
        outs = []
        for j in range(ATTN_HEADS):
            h = j // ATTN_GROUP
            q = q_s[rows, j * HEAD_DIM:(j + 1) * HEAD_DIM]
            s = _dot_t(q, k_s[h]) * scale
            p = _softmax_rows(s).astype(BF16)
            outs.append(_dot(p, v_s[h]))
        o_ref[rows, :] = jnp.concatenate(outs, axis=-1)


def _attn_a(aq, ak, av, gq, gk, ones_bd, *, B, T, row0, latent, rope=None, ctx_k=None, ctx_v=None):
    blk0 = row0 // T
    past = ctx_k.shape[2] if latent else 0
    rows = lambda w: pl.BlockSpec((T, w), lambda b: (blk0 + b, 0))
    const = lambda shp: pl.BlockSpec(shp, lambda b: (0,) * len(shp))
    in_specs = [rows(A_Q_W), rows(A_KV_W), rows(A_KV_W), const((1, A_Q_W)), const((1, A_KV_W)),
                const((A_Q_W, A_Q_W))]
    args = [aq, ak, av, gq, gk, ones_bd]
    if latent:
        in_specs += [const((T, LANES))] * 3
        in_specs += [pl.BlockSpec((1, ATTN_KV_HEADS, past, HEAD_DIM), lambda b: (b, 0, 0, 0))] * 2
        args += list(rope) + [ctx_k, ctx_v]
    return pl.pallas_call(
        functools.partial(_attn_a_kernel, latent=latent, T=T, past=past),
        out_shape=(jax.ShapeDtypeStruct((B * T, A_Q_W), F32), jax.ShapeDtypeStruct((B * T, A_KV_W), F32)),
        grid=(B,),
        in_specs=in_specs,
        out_specs=(pl.BlockSpec((T, A_Q_W), lambda b: (b, 0)), pl.BlockSpec((T, A_KV_W), lambda b: (b, 0))),
        scratch_shapes=[pltpu.VMEM((T, A_Q_W), BF16),
                        pltpu.VMEM((ATTN_KV_HEADS, T + past, HEAD_DIM), BF16),
                        pltpu.VMEM((ATTN_KV_HEADS, T + past, HEAD_DIM), BF16)],
        compiler_params=_cparams("arbitrary"),
        name="attn_gqa_latent" if latent else "attn_gqa_context",
    )(*args)


def _attn_b_kernel(*refs, latent, T, past, lam_init):
    o_ref, k_s, v_s = refs[-3:]
    if latent:
        bq_ref, bk_ref, bv_ref, lv_ref, sg_ref, cos_ref, sup_ref, sdn_ref, ck_ref, cv_ref = refs[:10]
    else:
        bq_ref, bk_ref, bv_ref, lv_ref, sg_ref = refs[:5]
    lv = lv_ref[...]
    d01 = jnp.sum(lv[0:1, :] * lv[1:2, :], axis=-1, keepdims=True)
    d23 = jnp.sum(lv[2:3, :] * lv[3:4, :], axis=-1, keepdims=True)
    lam = jnp.exp(d01) - jnp.exp(d23) + lam_init
    q = bq_ref[...]
    k = bk_ref[...]
    if latent:
        cos, sup, sdn = cos_ref[...], sup_ref[...], sdn_ref[...]
        q = _rope128(q, cos, sup, sdn)
        k = _rope128(k, cos, sup, sdn)
    q = q.astype(BF16)
    k_s[0:T, :] = k.astype(BF16)
    v_s[0:T, :] = bv_ref[...].astype(BF16)
    if latent:
        k_s[T:T + past, :] = ck_ref[0, 0].astype(BF16)
        v_s[T:T + past, :] = cv_ref[0, 0].astype(BF16)
    scale = HEAD_DIM ** -0.5
    inv_d = 1.0 / (2 * HEAD_DIM)
    k1 = k_s[:, 0:HEAD_DIM]
    k2 = k_s[:, HEAD_DIM:2 * HEAD_DIM]
    v = v_s[...]
    for qb in range(T // Q_TILE):
        rows = slice(qb * Q_TILE, (qb + 1) * Q_TILE)
        s1 = _dot_t(q[rows, 0:HEAD_DIM], k1) * scale
        s2 = _dot_t(q[rows, HEAD_DIM:2 * HEAD_DIM], k2) * scale
        p = (_softmax_rows(s1) - lam * _softmax_rows(s2)).astype(BF16)
        o = _dot(p, v)
        ms = jnp.mean(o * o, axis=-1, keepdims=True)
        o_ref[rows, :] = o * lax.rsqrt(ms + RMS_EPS) * sg_ref[...] * (1.0 - lam_init)


def _attn_b(bq, bk, bv, lam_vec, subln_g, *, B, T, row0, latent, lam_init, rope=None, ctx_k=None, ctx_v=None):
    blk0 = row0 // T
    past = ctx_k.shape[2] if latent else 0
    dh = 2 * HEAD_DIM
    rows = pl.BlockSpec((T, dh), lambda b, h: (blk0 + b, h))
    const = lambda shp: pl.BlockSpec(shp, lambda b, h: (0,) * len(shp))
    in_specs = [rows, rows, rows, const((4, HEAD_DIM)), const((1, dh))]
    args = [bq, bk, bv, lam_vec, subln_g]
    if latent:
        in_specs += [const((T, LANES))] * 3
        in_specs += [pl.BlockSpec((1, 1, past, dh), lambda b, h: (b, h, 0, 0))] * 2
        args += list(rope) + [ctx_k, ctx_v]
    return pl.pallas_call(
        functools.partial(_attn_b_kernel, latent=latent, T=T, past=past, lam_init=lam_init),
        out_shape=jax.ShapeDtypeStruct((B * T, B_V_W), F32),
        grid=(B, DIFF_HEADS),
        in_specs=in_specs,
        out_specs=pl.BlockSpec((T, dh), lambda b, h: (b, h)),
        scratch_shapes=[pltpu.VMEM((T + past, dh), BF16), pltpu.VMEM((T + past, dh), BF16)],
        compiler_params=_cparams("arbitrary", "arbitrary"),
        name="attn_diff_latent" if latent else "attn_diff_context",
    )(*args)


def _rwkv_prep_kernel(cin_ref, prev_ref, next_ref, mu_ref, w0_ref, w2_ref, a0_ref, a2_ref, g2_ref,
                      ka_ref, rk_ref, ones_ref, *rest, tm, per_seq):
    r_o, k_o, v_o, w_o, a_o, g_o, bonus_o = rest[-7:]
    pos = pl.program_id(0) % per_seq
    row_before = jnp.where(pos == 0, 0.0, prev_ref[SUBLANES - 1:SUBLANES, :])
    row_after = jnp.where(pos == per_seq - 1, 0.0, next_ref[0:1, :])
    x = cin_ref[...]
    t_idx = lax.broadcasted_iota(jnp.int32, (tm, 1), 0)
    prev = jnp.where(t_idx == 0, row_before, pltpu.roll(x, 1, axis=0))
    nxt = jnp.where(t_idx == tm - 1, row_after, pltpu.roll(x, tm - 1, axis=0))
    x = x + mu_ref[...] * (0.5 * (prev + nxt) - x)
    bw = BRANCH_WIDTH
    r = x[:, 0:bw]
    k = x[:, bw:2 * bw]
    v = x[:, 2 * bw:3 * bw]
    off = 3 * bw
    wl = x[:, off:off + 2 * W_LORA]
    al = x[:, off + 2 * W_LORA:off + 2 * W_LORA + 2 * A_LORA]
    gl = x[:, off + 2 * W_LORA + 2 * A_LORA:]
    ones = ones_ref[...]
    r_o[...] = r
    k_o[...] = k
    v_o[...] = v
    g_o[...] = _dot(_sigmoid(gl).astype(BF16), g2_ref[...])
    tw = jnp.tanh(wl).astype(BF16)
    alb = al.astype(BF16)
    decay_rate = math.exp(-0.5)
    bonus = jnp.zeros_like(r)
    for d in range(2):
        w_logit = w0_ref[d:d + 1, :] + _dot(tw[:, d * W_LORA:(d + 1) * W_LORA], w2_ref[d])
        w_o[d] = jnp.exp(-decay_rate * _sigmoid(w_logit))
        a = _sigmoid(a0_ref[d:d + 1, :] + _dot(alb[:, d * A_LORA:(d + 1) * A_LORA], a2_ref[d]))
        a_o[d] = a
        kc = k * (1.0 + (a - 1.0) * ka_ref[...])
        bonus = bonus + _seg_sum(r * kc * rk_ref[...], ones) * v
    bonus_o[...] = bonus


def _rwkv_prep(cin, p, ones_bd, *, row0, n_rows, t_seq):
    n = cin.shape[0]
    tm = ROW_TILE
    bw = BRANCH_WIDTH
    blk0 = row0 // tm
    halo = tm // SUBLANES
    last = n // SUBLANES - 1
    const = lambda shp: pl.BlockSpec(shp, lambda i: (0,) * len(shp))
    tok = jax.ShapeDtypeStruct((n_rows, bw), F32)
    tok2 = jax.ShapeDtypeStruct((2, n_rows, bw), F32)
    tspec = pl.BlockSpec((tm, bw), lambda i: (i, 0))
    tspec2 = pl.BlockSpec((2, tm, bw), lambda i: (0, i, 0))
    in_specs = [pl.BlockSpec((tm, C_IN_W), lambda i: (blk0 + i, 0)),
                pl.BlockSpec((SUBLANES, C_IN_W), lambda i: (jnp.maximum((blk0 + i) * halo - 1, 0), 0)),
                pl.BlockSpec((SUBLANES, C_IN_W), lambda i: (jnp.minimum((blk0 + i + 1) * halo, last), 0)),
                const((1, C_IN_W)), const((2, bw)), const((2, W_LORA, bw)), const((2, bw)),
                const((2, A_LORA, bw)), const((G_LORA, bw)), const((1, bw)), const((1, bw)),
                const((bw, bw))]
    args = [cin, cin, cin, p['mu'], p['w0'], p['w2'], p['a0'], p['a2'], p['g2'], p['k_a'], p['r_k'], ones_bd]
    return pl.pallas_call(
        functools.partial(_rwkv_prep_kernel, tm=tm, per_seq=t_seq // tm),
        out_shape=(tok, tok, tok, tok2, tok2, tok, tok),
        grid=(n_rows // tm,),
        in_specs=in_specs,
        out_specs=(tspec, tspec, tspec, tspec2, tspec2, tspec, tspec),
        compiler_params=_cparams("arbitrary"),
        name="rwkv_streams",
    )(*args)


def _scan_kernel(*refs, tc, nv, dir_lanes):
    if dir_lanes:
        (rf, kf, vf, wf, af, rb, kb, vb, wb, ab, bwd_ref, kk_ref, ka_ref, s0_ref,
         of_ref, ob_ref, sT_ref, S, al_s, wr_s, be_s, kc_s, w_s, v_s, c_s) = refs
        bwd = bwd_ref[...] > 0.0
    else:
        (rf, kf, vf, wf, af, kk_ref, ka_ref, s0_ref, of_ref, sT_ref,
         S, al_s, wr_s, be_s, kc_s, w_s, v_s, c_s) = refs
    d = pl.program_id(0)
    j = pl.program_id(2)
    n = RWKV_HEAD_DIM

    @pl.when(j == 0)
    def _():
        S[...] = s0_ref[...]

    def derive(i, carry):
        if dir_lanes:
            ib = tc - 1 - i
            r = jnp.where(bwd, rb[ib], rf[i])
            k = jnp.where(bwd, kb[ib], kf[i])
            v = jnp.where(bwd, vb[ib], vf[i])
            w = jnp.where(bwd, wb[ib], wf[i])
            a = jnp.where(bwd, ab[ib], af[i])
        else:
            row = i + d * (tc - 1 - 2 * i)
            r, k, v, w, a = rf[row], kf[row], vf[row], wf[row], af[row]
        kkv = k * kk_ref[...]
        kk = kkv * lax.rsqrt(jnp.sum(kkv * kkv, axis=0, keepdims=True) + 1e-12)
        kc = k * (1.0 + (a - 1.0) * ka_ref[...])
        beta = kk * a
        al_s[i] = -kk
        wr_s[i] = w * r
        be_s[i] = beta
        kc_s[i] = kc
        w_s[i] = w
        v_s[i] = v
        c_s[i, 0:1, :] = jnp.sum(beta * r, axis=0, keepdims=True)
        c_s[i, 1:2, :] = jnp.sum(kc * r, axis=0, keepdims=True)
        return carry

    lax.fori_loop(0, tc, derive, 0)

    def step(i, carry):
        def body_a(kx, acc):
            sa, y = acc
            sk = S[kx]
            return sa + sk * al_s[i, pl.ds(kx, 1), :], y + sk * wr_s[i, pl.ds(kx, 1), :]

        zero = jnp.zeros((nv, LANES), F32)
        sa, y = lax.fori_loop(0, n, body_a, (zero, zero), unroll=SCAN_UNROLL)
        v = v_s[i]

        def body_b(kx, c):
            S[kx] = (S[kx] * w_s[i, pl.ds(kx, 1), :] + be_s[i, pl.ds(kx, 1), :] * sa
                     + kc_s[i, pl.ds(kx, 1), :] * v)
            return c

        lax.fori_loop(0, n, body_b, 0, unroll=SCAN_UNROLL)
        out = y + sa * c_s[i, 0:1, :] + v * c_s[i, 1:2, :]
        if dir_lanes:
            of_ref[i] = out
            ob_ref[tc - 1 - i] = out
        else:
            of_ref[i + d * (tc - 1 - 2 * i)] = out
        return carry

    lax.fori_loop(0, tc, step, 0)

    @pl.when(j == pl.num_programs(2) - 1)
    def _():
        sT_ref[...] = S[...]


def _scan_scratch(nv):
    n = RWKV_HEAD_DIM
    tc = SCAN_TC
    return ([pltpu.VMEM((n, nv, LANES), F32)] + [pltpu.VMEM((tc, n, LANES), F32)] * 5
            + [pltpu.VMEM((tc, nv, LANES), F32), pltpu.VMEM((tc, SUBLANES, LANES), F32)])


def _rwkv_scan_groups(r, k, v, w, a, kk_t, ka_t, s0):
    LG, T = r.shape[:2]
    n = RWKV_HEAD_DIM
    tc = SCAN_TC
    nt = T // tc
    tb = lambda d, j: j + d * (nt - 1 - 2 * j)
    shared = pl.BlockSpec((None, tc, n, LANES), lambda d, g, j: (g, tb(d, j), 0, 0))
    perdir = pl.BlockSpec((None, None, tc, n, LANES), lambda d, g, j: (d, g, tb(d, j), 0, 0))
    par = pl.BlockSpec((n, LANES), lambda d, g, j: (0, 0))
    st = pl.BlockSpec((None, None, n, n, LANES), lambda d, g, j: (d, g, 0, 0, 0))
    return pl.pallas_call(
        functools.partial(_scan_kernel, tc=tc, nv=n, dir_lanes=False),
        out_shape=(jax.ShapeDtypeStruct((2, LG, T, n, LANES), F32), jax.ShapeDtypeStruct((2, LG, n, n, LANES), F32)),
        grid=(2, LG, nt),
        in_specs=[shared, shared, shared, perdir, perdir, par, par, st],
        out_specs=(perdir, st),
        scratch_shapes=_scan_scratch(n),
        compiler_params=_cparams("arbitrary", "arbitrary", "arbitrary"),
        name="rwkv_scan_context",
    )(r, k, v, w, a, kk_t, ka_t, s0)


def _rwkv_scan_lanes(r, k, v, w, a, bwd, kk_t, ka_t, s0):
    T = r.shape[0]
    nv = v.shape[1]
    n = RWKV_HEAD_DIM
    tc = SCAN_TC
    nt = T // tc
    fwd_k = pl.BlockSpec((tc, n, LANES), lambda d, g, j: (j, 0, 0))
    bwd_k = pl.BlockSpec((tc, n, LANES), lambda d, g, j: (nt - 1 - j, 0, 0))
    fwd_v = pl.BlockSpec((tc, nv, LANES), lambda d, g, j: (j, 0, 0))
    bwd_v = pl.BlockSpec((tc, nv, LANES), lambda d, g, j: (nt - 1 - j, 0, 0))
    par = pl.BlockSpec((n, LANES), lambda d, g, j: (0, 0))
    st = pl.BlockSpec((n, nv, LANES), lambda d, g, j: (0, 0, 0))
    o_sds = jax.ShapeDtypeStruct((T, nv, LANES), F32)
    return pl.pallas_call(
        functools.partial(_scan_kernel, tc=tc, nv=nv, dir_lanes=True),
        out_shape=(o_sds, o_sds, jax.ShapeDtypeStruct((n, nv, LANES), F32)),
        grid=(1, 1, nt),
        in_specs=[fwd_k, fwd_k, fwd_v, fwd_k, fwd_k, bwd_k, bwd_k, bwd_v, bwd_k, bwd_k,
                  pl.BlockSpec((1, LANES), lambda d, g, j: (0, 0)), par, par, st],
        out_specs=(fwd_v, bwd_v, st),
        scratch_shapes=_scan_scratch(nv),
        compiler_params=_cparams("arbitrary", "arbitrary", "arbitrary"),
        name="rwkv_scan_latent",
    )(r, k, v, w, a, r, k, v, w, a, bwd, kk_t, ka_t, s0)


def _to_lanes(x, B, T):
    lead = x.shape[:-2]
    x = x.reshape(lead + (B, T, RWKV_HEADS, RWKV_HEAD_DIM))
    nl = len(lead)
    perm = tuple(range(nl)) + (nl + 1, nl + 3, nl + 0, nl + 2)
    return x.transpose(perm).reshape(lead + (T, RWKV_HEAD_DIM, B * RWKV_HEADS))


def _lane_param(p, vl):
    t = jnp.repeat(p.reshape(RWKV_HEADS, RWKV_HEAD_DIM).T, vl, axis=1)
    return jnp.tile(t, (1, LANES // (RWKV_HEADS * vl)))


def _rwkv_context(r, k, v, w, a, k_k, k_a, B, T):
    n = RWKV_HEAD_DIM
    nb = B * RWKV_HEADS
    LG = -(-nb // LANES)

    def groups(x):
        f = _to_lanes(x, B, T)
        f = jnp.pad(f, [(0, 0)] * (f.ndim - 1) + [(0, LG * LANES - nb)])
        f = f.reshape(f.shape[:-1] + (LG, LANES))
        return jnp.moveaxis(f, -2, -4)

    s0 = jnp.zeros((2, LG, n, n, LANES), F32)
    o, sT = _rwkv_scan_groups(groups(r), groups(k), groups(v), groups(w), groups(a),
                              _lane_param(k_k, 1), _lane_param(k_a, 1), s0)
    o = jnp.moveaxis(o, 1, 3).reshape(2, T, n, LG * LANES)[..., :nb]
    o = o.reshape(2, T, n, B, RWKV_HEADS).transpose(0, 3, 1, 4, 2).reshape(2, B * T, BRANCH_WIDTH)
    s = jnp.moveaxis(sT, 1, 3).reshape(2, n, n, LG * LANES)[..., :nb]
    s = s.reshape(2, n, n, B, RWKV_HEADS).transpose(3, 0, 4, 2, 1)
    return o[0], o[1], s


def _rwkv_latent(r, k, v, w, a, k_k, k_a, state0, B, T):
    n = RWKV_HEAD_DIM
    H = RWKV_HEADS
    vl = min(SUBLANES, LANES // (2 * B * H))
    assert vl >= 1 and vl & (vl - 1) == 0
    nv = n // vl
    half = B * H * vl
    pad = lambda x: jnp.pad(x, [(0, 0)] * (x.ndim - 1) + [(0, LANES - 2 * half)])

    def k_lanes(x):
        return jnp.repeat(_to_lanes(x, B, T), vl, axis=-1)

    def shared_k(x):
        f = k_lanes(x)
        return pad(jnp.concatenate([f, f], axis=-1))

    def per_dir(x):
        f = k_lanes(x)
        return pad(jnp.concatenate([f[0], f[1]], axis=-1))

    vv = v.reshape(B, T, H, vl, nv).transpose(1, 4, 0, 2, 3).reshape(T, nv, half)
    vv = pad(jnp.concatenate([vv, vv], axis=-1))
    s0 = state0.reshape(B, 2, H, vl, nv, n).transpose(5, 4, 1, 0, 2, 3).reshape(n, nv, 2 * half)
    bwd = (jnp.arange(LANES) >= half).astype(F32)[None]
    o_f, o_b, _ = _rwkv_scan_lanes(shared_k(r), shared_k(k), vv, per_dir(w), per_dir(a), bwd,
                                   _lane_param(k_k, vl), _lane_param(k_a, vl), pad(s0))
    tok = lambda o: o.reshape(T, nv, B, H, vl).transpose(2, 0, 3, 4, 1).reshape(B * T, BRANCH_WIDTH)
    return tok(o_f[..., :half]), tok(o_b[..., half:2 * half])


def _merge_kernel(x_ref, mod_ref, gates_ref, *refs, alpha, n_prompt_blocks):
    p_refs, s_refs = refs[0:6], refs[6:12]
    gng_ref, gnb_ref, ones_ref, wup_ref, wout_ref, lng_ref, lnb_ref, o_ref = refs[12:]
    in_prompt = pl.program_id(0) < n_prompt_blocks
    oa, ob, sf, sb, g, bonus = [jnp.where(in_prompt, p[...], s[...]) for p, s in zip(p_refs, s_refs)]
    ones = ones_ref[...]
    inv_n = 1.0 / RWKV_HEAD_DIM
    o = sf + sb
    mu = _seg_sum(o, ones) * inv_n
    d = o - mu
    var = _seg_sum(d * d, ones) * inv_n
    oc = d * lax.rsqrt(var + RWKV_GN_EPS) * gng_ref[...] + gnb_ref[...]
    oc = (oc + bonus) * g
    branches = (oa, ob, oc)
    mix = None
    for i in range(N_BRANCH):
        up = _dot(branches[i].astype(BF16), wup_ref[i])
        term = _sigmoid(gates_ref[:, i * D_MODEL:(i + 1) * D_MODEL]) * up
        mix = term if mix is None else mix + term
    h = _dot(mix.astype(BF16), wout_ref[...])
    x = x_ref[...]
    y = alpha * x + mod_ref[0, 5:6, :] * h
    o_ref[...] = _layer_norm(y, lng_ref[...], lnb_ref[...])


def _merge_sublayer(x, mod, gates, prompt_parts, sample_parts, gn_g, gn_b, ones_bd, w_up, w_out,
                    ln_g, ln_b, *, alpha, row_map):
    n = x.shape[0]
    tm = ROW_TILE
    bw = BRANCH_WIDTH
    npb = prompt_parts[0].shape[0] // tm
    rows = lambda w: pl.BlockSpec((tm, w), lambda i: (i, 0))
    prow = pl.BlockSpec((tm, bw), lambda i: (jnp.minimum(i, npb - 1), 0))
    srow = pl.BlockSpec((tm, bw), lambda i: (jnp.maximum(i - npb, 0), 0))
    const = lambda shp: pl.BlockSpec(shp, lambda i: (0,) * len(shp))
    return pl.pallas_call(
        functools.partial(_merge_kernel, alpha=alpha, n_prompt_blocks=npb),
        out_shape=jax.ShapeDtypeStruct((n, D_MODEL), F32),
        grid=(n // tm,),
        in_specs=[rows(D_MODEL), pl.BlockSpec((1, N_MOD, D_MODEL), lambda i: (row_map(i), 0, 0)),
                  rows(N_BRANCH * D_MODEL)] + [prow] * 6 + [srow] * 6 + [
                  const((1, bw)), const((1, bw)), const((bw, bw)),
                  const((N_BRANCH, bw, D_MODEL)), const((D_MODEL, D_MODEL)),
                  const((1, D_MODEL)), const((1, D_MODEL))],
        out_specs=rows(D_MODEL),
        compiler_params=_cparams("arbitrary"),
        name="merge_sublayer",
    )(x, mod, gates, *prompt_parts, *sample_parts, gn_g, gn_b, ones_bd, w_up, w_out,
      ln_g.reshape(1, D_MODEL), ln_b.reshape(1, D_MODEL))


def _heads_first(x, B, T, H):
    return x.reshape(B, T, H, x.shape[-1] // H).transpose(0, 2, 1, 3)


def kernel(x_prompt, x_sample, cache_attn_k, cache_attn_v, cache_diff_k, cache_diff_v, state_rwkv, c, c_ctx,
           w_mod, b_mod, ln_g, ln_b, ffn_w_in, ffn_w_out, w_in, qk_norm_g, diff_lambda, diff_subln_g,
           rwkv_mu, rwkv_w0, rwkv_w2, rwkv_a0, rwkv_a2, rwkv_g2, rwkv_k_k, rwkv_k_a, rwkv_r_k,
           rwkv_gn_g, rwkv_gn_b, w_up, w_out):
    Bp, Tp, _ = x_prompt.shape
    Bs, Ts, _ = x_sample.shape
    depth = w_mod.shape[0]
    alpha = (2.0 * depth) ** 0.25
    n_p, n_s = Bp * Tp, Bs * Ts
    assert n_p % ROW_TILE == 0 and Ts % ROW_TILE == 0 and 1 + Bs <= MOD_ROWS
    row_map = _mod_row_map(n_p // ROW_TILE, Ts // ROW_TILE)

    x = jnp.concatenate([x_prompt.reshape(n_p, D_MODEL), x_sample.reshape(n_s, D_MODEL)], axis=0)
    cvecs = jnp.concatenate([c_ctx[None], c, jnp.zeros((MOD_ROWS - 1 - Bs, D_MODEL), F32)], axis=0)
    mod_all = _modulation(cvecs, w_mod, b_mod)

    seg = np.arange(BRANCH_WIDTH) // HEAD_DIM
    ones_bd = jnp.asarray(seg[:, None] == seg[None, :], BF16)
    rope = _rope_tables(Ts)
    bw = BRANCH_WIDTH

    n_ak, n_av, n_dk, n_dv, n_st = [], [], [], [], []
    for l in range(depth):
        lam_init = 0.8 - 0.6 * math.exp(-0.3 * l)
        mod = mod_all[l]
        x = _ffn_sublayer(x, mod, ffn_w_in[l, 0].astype(BF16), ffn_w_out[l, 0].astype(BF16),
                          ln_g[l, 0], ln_b[l, 0], sub=0, alpha=alpha, row_map=row_map)
        gates, aq, ak, av, bq, bk, bv, cin = _in_projection(x, mod, w_in[l].astype(BF16), row_map=row_map)

        gq = jnp.tile(qk_norm_g[l, 0], ATTN_HEADS)[None]
        gk = jnp.tile(qk_norm_g[l, 1], ATTN_KV_HEADS)[None]
        oa_p, kn_p = _attn_a(aq, ak, av, gq, gk, ones_bd, B=Bp, T=Tp, row0=0, latent=False)
        oa_s, _ = _attn_a(aq, ak, av, gq, gk, ones_bd, B=Bs, T=Ts, row0=n_p, latent=True, rope=rope,
                          ctx_k=cache_attn_k[:, l], ctx_v=cache_attn_v[:, l])
        sg = diff_subln_g[l][None]
        ob_p = _attn_b(bq, bk, bv, diff_lambda[l], sg, B=Bp, T=Tp, row0=0, latent=False, lam_init=lam_init)
        ob_s = _attn_b(bq, bk, bv, diff_lambda[l], sg, B=Bs, T=Ts, row0=n_p, latent=True, lam_init=lam_init,
                       rope=rope, ctx_k=cache_diff_k[:, l], ctx_v=cache_diff_v[:, l])

        rp = {'mu': rwkv_mu[l][None], 'w0': rwkv_w0[l], 'w2': rwkv_w2[l].astype(BF16), 'a0': rwkv_a0[l],
              'a2': rwkv_a2[l].astype(BF16), 'g2': rwkv_g2[l].astype(BF16),
              'k_a': rwkv_k_a[l][None], 'r_k': rwkv_r_k[l].reshape(1, bw)}
        r, k, v, w, a, g_p, bonus_p = _rwkv_prep(cin, rp, ones_bd, row0=0, n_rows=n_p, t_seq=Tp)
        sf_p, sb_p, st = _rwkv_context(r, k, v, w, a, rwkv_k_k[l], rwkv_k_a[l], Bp, Tp)
        r, k, v, w, a, g_s, bonus_s = _rwkv_prep(cin, rp, ones_bd, row0=n_p, n_rows=n_s, t_seq=Ts)
        sf_s, sb_s = _rwkv_latent(r, k, v, w, a, rwkv_k_k[l], rwkv_k_a[l], state_rwkv[:, l], Bs, Ts)
        n_st.append(st)
        x = _merge_sublayer(x, mod, gates, (oa_p, ob_p, sf_p, sb_p, g_p, bonus_p),
                            (oa_s, ob_s, sf_s, sb_s, g_s, bonus_s),
                            rwkv_gn_g[l][None], rwkv_gn_b[l][None], ones_bd,
                            w_up[l].astype(BF16), w_out[l].astype(BF16), ln_g[l, 1], ln_b[l, 1],
                            alpha=alpha, row_map=row_map)
        x = _ffn_sublayer(x, mod, ffn_w_in[l, 1].astype(BF16), ffn_w_out[l, 1].astype(BF16),
                          ln_g[l, 2], ln_b[l, 2], sub=2, alpha=alpha, row_map=row_map)

        n_ak.append(_heads_first(kn_p, Bp, Tp, ATTN_KV_HEADS))
        n_av.append(_heads_first(av[:n_p], Bp, Tp, ATTN_KV_HEADS))
        n_dk.append(_heads_first(bk[:n_p], Bp, Tp, DIFF_HEADS))
        n_dv.append(_heads_first(bv[:n_p], Bp, Tp, DIFF_HEADS))

    y_prompt = x[:n_p].reshape(Bp, Tp, D_MODEL)
    y_sample = x[n_p:].reshape(Bs, Ts, D_MODEL)
    stack = lambda parts: jnp.stack(parts, axis=1)
    return (y_prompt, y_sample, stack(n_ak), stack(n_av), stack(n_dk), stack(n_dv), stack(n_st))
```

```python
import functools
import math

import numpy as np
import jax
import jax.numpy as jnp
from jax import lax
from jax.experimental import pallas as pl
from jax.experimental.pallas import tpu as pltpu

F32 = jnp.float32
BF16 = jnp.bfloat16

D_MODEL = 1024
GRID_W = 64
HEAD_DIM = 64
ATTN_HEADS = 8
ATTN_KV_HEADS = 2
ATTN_GROUP = ATTN_HEADS // ATTN_KV_HEADS
DIFF_HEADS = 4
RWKV_HEADS = 8
RWKV_HEAD_DIM = 64
BRANCH_WIDTH = 512
W_LORA = 64
A_LORA = 64
G_LORA = 128
D_FF = 2816
N_BRANCH = 3
N_MOD = 9
ROPE_THETA = 10000.0
ROPE_AXIS_DIM = HEAD_DIM // 2
LN_EPS = 1e-5
RMS_EPS = 1e-6
RWKV_GN_EPS = 64e-5
A_Q_W = ATTN_HEADS * HEAD_DIM
A_KV_W = ATTN_KV_HEADS * HEAD_DIM
B_QK_W = DIFF_HEADS * 2 * HEAD_DIM
B_V_W = DIFF_HEADS * 2 * HEAD_DIM
C_IN_W = 3 * BRANCH_WIDTH + 2 * W_LORA + 2 * A_LORA + G_LORA
IN_SPLITS = (N_BRANCH * D_MODEL, A_Q_W, A_KV_W, A_KV_W, B_QK_W, B_QK_W, B_V_W, C_IN_W)
IN_WIDTH = sum(IN_SPLITS)

LANES = 128
SUBLANES = 8
VMEM_LIMIT_BYTES = 56 * 1024 * 1024

MOD_ROWS = 8
MOD_TN = 1152
FFN_CK = 256
ROW_TILE = 256
Q_TILE = 256
SCAN_TC = 32
SCAN_UNROLL = 16


def _cparams(*sem):
    return pltpu.CompilerParams(dimension_semantics=sem, vmem_limit_bytes=VMEM_LIMIT_BYTES)


def _sigmoid(x):
    return 1.0 / (1.0 + jnp.exp(-x))


def _dot(a, b):
    return jnp.dot(a, b, preferred_element_type=F32)


def _dot_t(a, b):
    return lax.dot_general(a, b, (((1,), (1,)), ((), ())), preferred_element_type=F32)


def _seg_sum(x, ones_bd):
    hi = x.astype(BF16)
    r1 = x - hi.astype(F32)
    mid = r1.astype(BF16)
    lo = (r1 - mid.astype(F32)).astype(BF16)
    return _dot(hi, ones_bd) + _dot(mid, ones_bd) + _dot(lo, ones_bd)


def _layer_norm(y, g, b):
    mu = jnp.mean(y, axis=-1, keepdims=True)
    d = y - mu
    var = jnp.mean(d * d, axis=-1, keepdims=True)
    return d * lax.rsqrt(var + LN_EPS) * g + b


def _mod_row_map(n_prompt_blocks, blocks_per_sample):
    def row(i):
        return jnp.where(i < n_prompt_blocks, 0, 1 + (i - n_prompt_blocks) // blocks_per_sample)
    return row


def _mod_kernel(c_ref, w_ref, b_ref, o_ref):
    cv = c_ref[...]
    h = (cv * _sigmoid(cv)).astype(BF16)
    o_ref[0] = _dot(h, w_ref[0].astype(BF16)) + b_ref[0]


def _modulation(cvecs, w_mod, b_mod):
    L = w_mod.shape[0]
    n = N_MOD * D_MODEL
    out = pl.pallas_call(
        _mod_kernel,
        out_shape=jax.ShapeDtypeStruct((L, MOD_ROWS, n), F32),
        grid=(L, n // MOD_TN),
        in_specs=[
            pl.BlockSpec((MOD_ROWS, D_MODEL), lambda l, j: (0, 0)),
            pl.BlockSpec((1, D_MODEL, MOD_TN), lambda l, j: (l, 0, j)),
            pl.BlockSpec((1, 1, MOD_TN), lambda l, j: (l, 0, j)),
        ],
        out_specs=pl.BlockSpec((1, MOD_ROWS, MOD_TN), lambda l, j: (l, 0, j)),
        compiler_params=_cparams("arbitrary", "arbitrary"),
        name="modulation",
    )(cvecs, w_mod, b_mod.reshape(L, 1, n))
    return out.reshape(L, MOD_ROWS, N_MOD, D_MODEL)


def _ffn_kernel(x_ref, mod_ref, win_ref, wout_ref, g_ref, b_ref, o_ref, act_ref, *, sub, alpha):
    x = x_ref[...]
    shift = mod_ref[0, 3 * sub:3 * sub + 1, :]
    scale = mod_ref[0, 3 * sub + 1:3 * sub + 2, :]
    gate = mod_ref[0, 3 * sub + 2:3 * sub + 3, :]
    h = (x * (1.0 + scale) + shift).astype(BF16)
    for c in range(D_FF // FFN_CK):
        a = _dot(h, win_ref[:, c * FFN_CK:(c + 1) * FFN_CK])
        b = _dot(h, win_ref[:, D_FF + c * FFN_CK:D_FF + (c + 1) * FFN_CK])
        act_ref[:, c * FFN_CK:(c + 1) * FFN_CK] = (a * _sigmoid(a) * b).astype(BF16)
    f = _dot(act_ref[...], wout_ref[...])
    y = alpha * x + (0.5 * gate) * f
    o_ref[...] = _layer_norm(y, g_ref[...], b_ref[...])


def _ffn_sublayer(x, mod, w_in, w_out, ln_g, ln_b, *, sub, alpha, row_map):
    n = x.shape[0]
    tm = ROW_TILE
    const = lambda i: (0, 0)
    return pl.pallas_call(
        functools.partial(_ffn_kernel, sub=sub, alpha=alpha),
        out_shape=jax.ShapeDtypeStruct((n, D_MODEL), F32),
        grid=(n // tm,),
        in_specs=[
            pl.BlockSpec((tm, D_MODEL), lambda i: (i, 0)),
            pl.BlockSpec((1, N_MOD, D_MODEL), lambda i: (row_map(i), 0, 0)),
            pl.BlockSpec((D_MODEL, 2 * D_FF), const),
            pl.BlockSpec((D_FF, D_MODEL), const),
            pl.BlockSpec((1, D_MODEL), const),
            pl.BlockSpec((1, D_MODEL), const),
        ],
        out_specs=pl.BlockSpec((tm, D_MODEL), lambda i: (i, 0)),
        scratch_shapes=[pltpu.VMEM((tm, D_FF), BF16)],
        compiler_params=_cparams("arbitrary"),
        name="ffn_sublayer",
    )(x, mod, w_in, w_out, ln_g.reshape(1, D_MODEL), ln_b.reshape(1, D_MODEL))


def _inproj_kernel(x_ref, mod_ref, w_ref, *out_refs):
    x = x_ref[...]
    shift = mod_ref[0, 3:4, :]
    scale = mod_ref[0, 4:5, :]
    h = (x * (1.0 + scale) + shift).astype(BF16)
    off = 0
    for o_ref, width in zip(out_refs, IN_SPLITS):
        o_ref[...] = _dot(h, w_ref[:, off:off + width])
        off += width


def _in_projection(x, mod, w_in, *, row_map):
    n = x.shape[0]
    tm = ROW_TILE
    return pl.pallas_call(
        _inproj_kernel,
        out_shape=tuple(jax.ShapeDtypeStruct((n, w), F32) for w in IN_SPLITS),
        grid=(n // tm,),
        in_specs=[
            pl.BlockSpec((tm, D_MODEL), lambda i: (i, 0)),
            pl.BlockSpec((1, N_MOD, D_MODEL), lambda i: (row_map(i), 0, 0)),
            pl.BlockSpec((D_MODEL, IN_WIDTH), lambda i: (0, 0)),
        ],
        out_specs=tuple(pl.BlockSpec((tm, w), lambda i: (i, 0)) for w in IN_SPLITS),
        compiler_params=_cparams("arbitrary"),
        name="in_projection",
    )(x, mod, w_in)


def _rope_tables(T):
    n_rows = T // GRID_W
    row = np.repeat(np.arange(n_rows), GRID_W).astype(np.float32)
    col = np.tile(np.arange(GRID_W), n_rows).astype(np.float32)
    inv = (1.0 / (ROPE_THETA ** (np.arange(0, ROPE_AXIS_DIM, 2, dtype=np.float32) / ROPE_AXIS_DIM))).astype(np.float32)
    ang_r = row[:, None] * inv
    ang_c = col[:, None] * inv
    z = np.zeros_like(ang_r)
    cos = np.concatenate([np.cos(ang_r), np.cos(ang_r), np.cos(ang_c), np.cos(ang_c)], axis=1)
    s_up = np.concatenate([-np.sin(ang_r), z, -np.sin(ang_c), z], axis=1)
    s_dn = np.concatenate([z, np.sin(ang_r), z, np.sin(ang_c)], axis=1)
    tile2 = lambda t: jnp.asarray(np.concatenate([t, t], axis=1), F32)
    return tile2(cos), tile2(s_up), tile2(s_dn)


def _rope128(x, cos, s_up, s_dn):
    up = pltpu.roll(x, LANES - ROPE_AXIS_DIM // 2, axis=1)
    dn = pltpu.roll(x, ROPE_AXIS_DIM // 2, axis=1)
    return x * cos + up * s_up + dn * s_dn


def _softmax_rows(s):
    m = jnp.max(s, axis=-1, keepdims=True)
    e = jnp.exp(s - m)
    return e / jnp.sum(e, axis=-1, keepdims=True)


def _attn_a_kernel(*refs, latent, T, past):
    o_ref, kn_ref, q_s, k_s, v_s = refs[-5:]
    if latent:
        aq_ref, ak_ref, av_ref, gq_ref, gk_ref, ones_ref, cos_ref, sup_ref, sdn_ref, ck_ref, cv_ref = refs[:11]
    else:
        aq_ref, ak_ref, av_ref, gq_ref, gk_ref, ones_ref = refs[:6]
    ones = ones_ref[...]
    inv_d = 1.0 / HEAD_DIM
    aq = aq_ref[...]
    qn = aq * lax.rsqrt(_seg_sum(aq * aq, ones) * inv_d + RMS_EPS) * gq_ref[...]
    ak = ak_ref[...]
    kn = ak * lax.rsqrt(_seg_sum(ak * ak, ones[:A_KV_W, :A_KV_W]) * inv_d + RMS_EPS) * gk_ref[...]
    kn_ref[...] = kn
    if latent:
        cos, sup, sdn = cos_ref[...], sup_ref[...], sdn_ref[...]
        for j in range(A_Q_W // LANES):
            q_s[:, j * LANES:(j + 1) * LANES] = _rope128(
                qn[:, j * LANES:(j + 1) * LANES], cos, sup, sdn).astype(BF16)
        kr = _rope128(kn, cos, sup, sdn).astype(BF16)
    else:
        q_s[...] = qn.astype(BF16)
        kr = kn.astype(BF16)
    av = av_ref[...].astype(BF16)
    for h in range(ATTN_KV_HEADS):
        k_s[h, 0:T, :] = kr[:, h * HEAD_DIM:(h + 1) * HEAD_DIM]
        v_s[h, 0:T, :] = av[:, h * HEAD_DIM:(h + 1) * HEAD_DIM]
        if latent:
            k_s[h, T:T + past, :] = ck_ref[0, h].astype(BF16)
            v_s[h, T:T + past, :] = cv_ref[0, h].astype(BF16)
    scale = HEAD_DIM ** -0.5
    for qb in range(T // Q_TILE):
        rows = slice(qb * Q_TILE, (qb + 1) * Q_TILE)
        outs = []
        for j in range(ATTN_HEADS):
            h = j // ATTN_GROUP
            q = q_s[rows, j * HEAD_DIM:(j + 1) * HEAD_DIM]
            s = _dot_t(q, k_s[h]) * scale
            p = _softmax_rows(s).astype(BF16)
            outs.append(_dot(p, v_s[h]))
        o_ref[rows, :] = jnp.concatenate(outs, axis=-1)


def _attn_a(aq, ak, av, gq, gk, ones_bd, *, B, T, row0, latent, rope=None, ctx_k=None, ctx_v=None):
    blk0 = row0 // T
    past = ctx_k.shape[2] if latent else 0
    rows = lambda w: pl.BlockSpec((T, w), lambda b: (blk0 + b, 0))
    const = lambda shp: pl.BlockSpec(shp, lambda b: (0,) * len(shp))
    in_specs = [rows(A_Q_W), rows(A_KV_W), rows(A_KV_W), const((1, A_Q_W)), const((1, A_KV_W)),
                const((A_Q_W, A_Q_W))]
    args = [aq, ak, av, gq, gk, ones_bd]
    if latent:
        in_specs += [const((T, LANES))] * 3
        in_specs += [pl.BlockSpec((1, ATTN_KV_HEADS, past, HEAD_DIM), lambda b: (b, 0, 0, 0))] * 2
        args += list(rope) + [ctx_k, ctx_v]
    return pl.pallas_call(
        functools.partial(_attn_a_kernel, latent=latent, T=T, past=past),
        out_shape=(jax.ShapeDtypeStruct((B * T, A_Q_W), F32), jax.ShapeDtypeStruct((B * T, A_KV_W), F32)),
        grid=(B,),
        in_specs=in_specs,
        out_specs=(pl.BlockSpec((T, A_Q_W), lambda b: (b, 0)), pl.BlockSpec((T, A_KV_W), lambda b: (b, 0))),
        scratch_shapes=[pltpu.VMEM((T, A_Q_W), BF16),
                        pltpu.VMEM((ATTN_KV_HEADS, T + past, HEAD_DIM), BF16),
                        pltpu.VMEM((ATTN_KV_HEADS, T + past, HEAD_DIM), BF16)],
        compiler_params=_cparams("arbitrary"),
        name="attn_gqa_latent" if latent else "attn_gqa_context",
    )(*args)


def _attn_b_kernel(*refs, latent, T, past, lam_init):
    o_ref, k_s, v_s = refs[-3:]
    if latent:
        bq_ref, bk_ref, bv_ref, lv_ref, sg_ref, cos_ref, sup_ref, sdn_ref, ck_ref, cv_ref = refs[:10]
    else:
        bq_ref, bk_ref, bv_ref, lv_ref, sg_ref = refs[:5]
    lv = lv_ref[...]
    d01 = jnp.sum(lv[0:1, :] * lv[1:2, :], axis=-1, keepdims=True)
    d23 = jnp.sum(lv[2:3, :] * lv[3:4, :], axis=-1, keepdims=True)
    lam = jnp.exp(d01) - jnp.exp(d23) + lam_init
    q = bq_ref[...]
    k = bk_ref[...]
    if latent:
        cos, sup, sdn = cos_ref[...], sup_ref[...], sdn_ref[...]
        q = _rope128(q, cos, sup, sdn)
        k = _rope128(k, cos, sup, sdn)
    q = q.astype(BF16)
    k_s[0:T, :] = k.astype(BF16)
    v_s[0:T, :] = bv_ref[...].astype(BF16)
    if latent:
        k_s[T:T + past, :] = ck_ref[0, 0].astype(BF16)
        v_s[T:T + past, :] = cv_ref[0, 0].astype(BF16)
    scale = HEAD_DIM ** -0.5
    inv_d = 1.0 / (2 * HEAD_DIM)
    k1 = k_s[:, 0:HEAD_DIM]
    k2 = k_s[:, HEAD_DIM:2 * HEAD_DIM]
    v = v_s[...]
    for qb in range(T // Q_TILE):
        rows = slice(qb * Q_TILE, (qb + 1) * Q_TILE)
        s1 = _dot_t(q[rows, 0:HEAD_DIM], k1) * scale
        s2 = _dot_t(q[rows, HEAD_DIM:2 * HEAD_DIM], k2) * scale
        p = (_softmax_rows(s1) - lam * _softmax_rows(s2)).astype(BF16)
        o = _dot(p, v)
        ms = jnp.mean(o * o, axis=-1, keepdims=True)
        o_ref[rows, :] = o * lax.rsqrt(ms + RMS_EPS) * sg_ref[...] * (1.0 - lam_init)


def _attn_b(bq, bk, bv, lam_vec, subln_g, *, B, T, row0, latent, lam_init, rope=None, ctx_k=None, ctx_v=None):
    blk0 = row0 // T
    past = ctx_k.shape[2] if latent else 0
    dh = 2 * HEAD_DIM
    rows = pl.BlockSpec((T, dh), lambda b, h: (blk0 + b, h))
    const = lambda shp: pl.BlockSpec(shp, lambda b, h: (0,) * len(shp))
    in_specs = [rows, rows, rows, const((4, HEAD_DIM)), const((1, dh))]
    args = [bq, bk, bv, lam_vec, subln_g]
    if latent:
        in_specs += [const((T, LANES))] * 3
        in_specs += [pl.BlockSpec((1, 1, past, dh), lambda b, h: (b, h, 0, 0))] * 2
        args += list(rope) + [ctx_k, ctx_v]
    return pl.pallas_call(
        functools.partial(_attn_b_kernel, latent=latent, T=T, past=past, lam_init=lam_init),
        out_shape=jax.ShapeDtypeStruct((B * T, B_V_W), F32),
        grid=(B, DIFF_HEADS),
        in_specs=in_specs,
        out_specs=pl.BlockSpec((T, dh), lambda b, h: (b, h)),
        scratch_shapes=[pltpu.VMEM((T + past, dh), BF16), pltpu.VMEM((T + past, dh), BF16)],
        compiler_params=_cparams("arbitrary", "arbitrary"),
        name="attn_diff_latent" if latent else "attn_diff_context",
    )(*args)


def _rwkv_prep_kernel(cin_ref, prev_ref, next_ref, mu_ref, w0_ref, w2_ref, a0_ref, a2_ref, g2_ref,
                      ka_ref, rk_ref, ones_ref, *rest, tm, per_seq):
    r_o, k_o, v_o, w_o, a_o, g_o, bonus_o = rest[-7:]
    pos = pl.program_id(0) % per_seq
    row_before = jnp.where(pos == 0, 0.0, prev_ref[SUBLANES - 1:SUBLANES, :])
    row_after = jnp.where(pos == per_seq - 1, 0.0, next_ref[0:1, :])
    x = cin_ref[...]
    t_idx = lax.broadcasted_iota(jnp.int32, (tm, 1), 0)
    prev = jnp.where(t_idx == 0, row_before, pltpu.roll(x, 1, axis=0))
    nxt = jnp.where(t_idx == tm - 1, row_after, pltpu.roll(x, tm - 1, axis=0))
    x = x + mu_ref[...] * (0.5 * (prev + nxt) - x)
    bw = BRANCH_WIDTH
    r = x[:, 0:bw]
    k = x[:, bw:2 * bw]
    v = x[:, 2 * bw:3 * bw]
    off = 3 * bw
    wl = x[:, off:off + 2 * W_LORA]
    al = x[:, off + 2 * W_LORA:off + 2 * W_LORA + 2 * A_LORA]
    gl = x[:, off + 2 * W_LORA + 2 * A_LORA:]
    ones = ones_ref[...]
    r_o[...] = r
    k_o[...] = k
    v_o[...] = v
    g_o[...] = _dot(_sigmoid(gl).astype(BF16), g2_ref[...])
    tw = jnp.tanh(wl).astype(BF16)
    alb = al.astype(BF16)
    decay_rate = math.exp(-0.5)
    bonus = jnp.zeros_like(r)
    for d in range(2):
        w_logit = w0_ref[d:d + 1, :] + _dot(tw[:, d * W_LORA:(d + 1) * W_LORA], w2_ref[d])
        w_o[d] = jnp.exp(-decay_rate * _sigmoid(w_logit))
        a = _sigmoid(a0_ref[d:d + 1, :] + _dot(alb[:, d * A_LORA:(d + 1) * A_LORA], a2_ref[d]))
        a_o[d] = a
        kc = k * (1.0 + (a - 1.0) * ka_ref[...])
        bonus = bonus + _seg_sum(r * kc * rk_ref[...], ones) * v
    bonus_o[...] = bonus


def _rwkv_prep(cin, p, ones_bd, *, row0, n_rows, t_seq):
    n = cin.shape[0]
    tm = ROW_TILE
    bw = BRANCH_WIDTH
    blk0 = row0 // tm
    halo = tm // SUBLANES
    last = n // SUBLANES - 1
    const = lambda shp: pl.BlockSpec(shp, lambda i: (0,) * len(shp))
    tok = jax.ShapeDtypeStruct((n_rows, bw), F32)
    tok2 = jax.ShapeDtypeStruct((2, n_rows, bw), F32)
    tspec = pl.BlockSpec((tm, bw), lambda i: (i, 0))
    tspec2 = pl.BlockSpec((2, tm, bw), lambda i: (0, i, 0))
    in_specs = [pl.BlockSpec((tm, C_IN_W), lambda i: (blk0 + i, 0)),
                pl.BlockSpec((SUBLANES, C_IN_W), lambda i: (jnp.maximum((blk0 + i) * halo - 1, 0), 0)),
                pl.BlockSpec((SUBLANES, C_IN_W), lambda i: (jnp.minimum((blk0 + i + 1) * halo, last), 0)),
                const((1, C_IN_W)), const((2, bw)), const((2, W_LORA, bw)), const((2, bw)),
                const((2, A_LORA, bw)), const((G_LORA, bw)), const((1, bw)), const((1, bw)),
                const((bw, bw))]
    args = [cin, cin, cin, p['mu'], p['w0'], p['w2'], p['a0'], p['a2'], p['g2'], p['k_a'], p['r_k'], ones_bd]
    return pl.pallas_call(
        functools.partial(_rwkv_prep_kernel, tm=tm, per_seq=t_seq // tm),
        out_shape=(tok, tok, tok, tok2, tok2, tok, tok),
        grid=(n_rows // tm,),
        in_specs=in_specs,
        out_specs=(tspec, tspec, tspec, tspec2, tspec2, tspec, tspec),
        compiler_params=_cparams("arbitrary"),
        name="rwkv_streams",
    )(*args)


def _scan_kernel(*refs, tc, nv, dir_lanes):
    if dir_lanes:
        (rf, kf, vf, wf, af, rb, kb, vb, wb, ab, bwd_ref, kk_ref, ka_ref, s0_ref,
         of_ref, ob_ref, sT_ref, S, al_s, wr_s, be_s, kc_s, w_s, v_s, c_s) = refs
        bwd = bwd_ref[...] > 0.0
    else:
        (rf, kf, vf, wf, af, kk_ref, ka_ref, s0_ref, of_ref, sT_ref,
         S, al_s, wr_s, be_s, kc_s, w_s, v_s, c_s) = refs
    d = pl.program_id(0)
    j = pl.program_id(2)
    n = RWKV_HEAD_DIM

    @pl.when(j == 0)
    def _():
        S[...] = s0_ref[...]

    def derive(i, carry):
        if dir_lanes:
            ib = tc - 1 - i
            r = jnp.where(bwd, rb[ib], rf[i])
            k = jnp.where(bwd, kb[ib], kf[i])
            v = jnp.where(bwd, vb[ib], vf[i])
            w = jnp.where(bwd, wb[ib], wf[i])
            a = jnp.where(bwd, ab[ib], af[i])
        else:
            row = i + d * (tc - 1 - 2 * i)
            r, k, v, w, a = rf[row], kf[row], vf[row], wf[row], af[row]
        kkv = k * kk_ref[...]
        kk = kkv * lax.rsqrt(jnp.sum(kkv * kkv, axis=0, keepdims=True) + 1e-12)
        kc = k * (1.0 + (a - 1.0) * ka_ref[...])
        beta = kk * a
        al_s[i] = -kk
        wr_s[i] = w * r
        be_s[i] = beta
        kc_s[i] = kc
        w_s[i] = w
        v_s[i] = v
        c_s[i, 0:1, :] = jnp.sum(beta * r, axis=0, keepdims=True)
        c_s[i, 1:2, :] = jnp.sum(kc * r, axis=0, keepdims=True)
        return carry

    lax.fori_loop(0, tc, derive, 0)

    def step(i, carry):
        def body_a(kx, acc):
            sa, y = acc
            sk = S[kx]
            return sa + sk * al_s[i, pl.ds(kx, 1), :], y + sk * wr_s[i, pl.ds(kx, 1), :]

        zero = jnp.zeros((nv, LANES), F32)
        sa, y = lax.fori_loop(0, n, body_a, (zero, zero), unroll=SCAN_UNROLL)
        v = v_s[i]

        def body_b(kx, c):
            S[kx] = (S[kx] * w_s[i, pl.ds(kx, 1), :] + be_s[i, pl.ds(kx, 1), :] * sa
                     + kc_s[i, pl.ds(kx, 1), :] * v)
            return c

        lax.fori_loop(0, n, body_b, 0, unroll=SCAN_UNROLL)
        out = y + sa * c_s[i, 0:1, :] + v * c_s[i, 1:2, :]
        if dir_lanes:
            of_ref[i] = out
            ob_ref[tc - 1 - i] = out
        else:
            of_ref[i + d * (tc - 1 - 2 * i)] = out
        return carry

    lax.fori_loop(0, tc, step, 0)

    @pl.when(j == pl.num_programs(2) - 1)
    def _():
        sT_ref[...] = S[...]


def _scan_scratch(nv):
    n = RWKV_HEAD_DIM
    tc = SCAN_TC
    return ([pltpu.VMEM((n, nv, LANES), F32)] + [pltpu.VMEM((tc, n, LANES), F32)] * 5
            + [pltpu.VMEM((tc, nv, LANES), F32), pltpu.VMEM((tc, SUBLANES, LANES), F32)])


def _rwkv_scan_groups(r, k, v, w, a, kk_t, ka_t, s0):
    LG, T = r.shape[:2]
    n = RWKV_HEAD_DIM
    tc = SCAN_TC
    nt = T // tc
    tb = lambda d, j: j + d * (nt - 1 - 2 * j)
    shared = pl.BlockSpec((None, tc, n, LANES), lambda d, g, j: (g, tb(d, j), 0, 0))
    perdir = pl.BlockSpec((None, None, tc, n, LANES), lambda d, g, j: (d, g, tb(d, j), 0, 0))
    par = pl.BlockSpec((n, LANES), lambda d, g, j: (0, 0))
    st = pl.BlockSpec((None, None, n, n, LANES), lambda d, g, j: (d, g, 0, 0, 0))
    return pl.pallas_call(
        functools.partial(_scan_kernel, tc=tc, nv=n, dir_lanes=False),
        out_shape=(jax.ShapeDtypeStruct((2, LG, T, n, LANES), F32), jax.ShapeDtypeStruct((2, LG, n, n, LANES), F32)),
        grid=(2, LG, nt),
        in_specs=[shared, shared, shared, perdir, perdir, par, par, st],
        out_specs=(perdir, st),
        scratch_shapes=_scan_scratch(n),
        compiler_params=_cparams("arbitrary", "arbitrary", "arbitrary"),
        name="rwkv_scan_context",
    )(r, k, v, w, a, kk_t, ka_t, s0)


def _rwkv_scan_lanes(r, k, v, w, a, bwd, kk_t, ka_t, s0):
    T = r.shape[0]
    nv = v.shape[1]
    n = RWKV_HEAD_DIM
    tc = SCAN_TC
    nt = T // tc
    fwd_k = pl.BlockSpec((tc, n, LANES), lambda d, g, j: (j, 0, 0))
    bwd_k = pl.BlockSpec((tc, n, LANES), lambda d, g, j: (nt - 1 - j, 0, 0))
    fwd_v = pl.BlockSpec((tc, nv, LANES), lambda d, g, j: (j, 0, 0))
    bwd_v = pl.BlockSpec((tc, nv, LANES), lambda d, g, j: (nt - 1 - j, 0, 0))
    par = pl.BlockSpec((n, LANES), lambda d, g, j: (0, 0))
    st = pl.BlockSpec((n, nv, LANES), lambda d, g, j: (0, 0, 0))
    o_sds = jax.ShapeDtypeStruct((T, nv, LANES), F32)
    return pl.pallas_call(
        functools.partial(_scan_kernel, tc=tc, nv=nv, dir_lanes=True),
        out_shape=(o_sds, o_sds, jax.ShapeDtypeStruct((n, nv, LANES), F32)),
        grid=(1, 1, nt),
        in_specs=[fwd_k, fwd_k, fwd_v, fwd_k, fwd_k, bwd_k, bwd_k, bwd_v, bwd_k, bwd_k,
                  pl.BlockSpec((1, LANES), lambda d, g, j: (0, 0)), par, par, st],
        out_specs=(fwd_v, bwd_v, st),
        scratch_shapes=_scan_scratch(nv),
        compiler_params=_cparams("arbitrary", "arbitrary", "arbitrary"),
        name="rwkv_scan_latent",
    )(r, k, v, w, a, r, k, v, w, a, bwd, kk_t, ka_t, s0)


def _to_lanes(x, B, T):
    lead = x.shape[:-2]
    x = x.reshape(lead + (B, T, RWKV_HEADS, RWKV_HEAD_DIM))
    nl = len(lead)
    perm = tuple(range(nl)) + (nl + 1, nl + 3, nl + 0, nl + 2)
    return x.transpose(perm).reshape(lead + (T, RWKV_HEAD_DIM, B * RWKV_HEADS))


def _lane_param(p, vl):
    t = jnp.repeat(p.reshape(RWKV_HEADS, RWKV_HEAD_DIM).T, vl, axis=1)
    return jnp.tile(t, (1, LANES // (RWKV_HEADS * vl)))


def _rwkv_context(r, k, v, w, a, k_k, k_a, B, T):
    n = RWKV_HEAD_DIM
    nb = B * RWKV_HEADS
    LG = -(-nb // LANES)

    def groups(x):
        f = _to_lanes(x, B, T)
        f = jnp.pad(f, [(0, 0)] * (f.ndim - 1) + [(0, LG * LANES - nb)])
        f = f.reshape(f.shape[:-1] + (LG, LANES))
        return jnp.moveaxis(f, -2, -4)

    s0 = jnp.zeros((2, LG, n, n, LANES), F32)
    o, sT = _rwkv_scan_groups(groups(r), groups(k), groups(v), groups(w), groups(a),
                              _lane_param(k_k, 1), _lane_param(k_a, 1), s0)
    o = jnp.moveaxis(o, 1, 3).reshape(2, T, n, LG * LANES)[..., :nb]
    o = o.reshape(2, T, n, B, RWKV_HEADS).transpose(0, 3, 1, 4, 2).reshape(2, B * T, BRANCH_WIDTH)
    s = jnp.moveaxis(sT, 1, 3).reshape(2, n, n, LG * LANES)[..., :nb]
    s = s.reshape(2, n, n, B, RWKV_HEADS).transpose(3, 0, 4, 2, 1)
    return o[0], o[1], s


def _rwkv_latent(r, k, v, w, a, k_k, k_a, state0, B, T):
    n = RWKV_HEAD_DIM
    H = RWKV_HEADS
    vl = min(SUBLANES, LANES // (2 * B * H))
    assert vl >= 1 and vl & (vl - 1) == 0
    nv = n // vl
    half = B * H * vl
    pad = lambda x: jnp.pad(x, [(0, 0)] * (x.ndim - 1) + [(0, LANES - 2 * half)])

    def k_lanes(x):
        return jnp.repeat(_to_lanes(x, B, T), vl, axis=-1)

    def shared_k(x):
        f = k_lanes(x)
        return pad(jnp.concatenate([f, f], axis=-1))

    def per_dir(x):
        f = k_lanes(x)
        return pad(jnp.concatenate([f[0], f[1]], axis=-1))

    vv = v.reshape(B, T, H, vl, nv).transpose(1, 4, 0, 2, 3).reshape(T, nv, half)
    vv = pad(jnp.concatenate([vv, vv], axis=-1))
    s0 = state0.reshape(B, 2, H, vl, nv, n).transpose(5, 4, 1, 0, 2, 3).reshape(n, nv, 2 * half)
    bwd = (jnp.arange(LANES) >= half).astype(F32)[None]
    o_f, o_b, _ = _rwkv_scan_lanes(shared_k(r), shared_k(k), vv, per_dir(w), per_dir(a), bwd,
                                   _lane_param(k_k, vl), _lane_param(k_a, vl), pad(s0))
    tok = lambda o: o.reshape(T, nv, B, H, vl).transpose(2, 0, 3, 4, 1).reshape(B * T, BRANCH_WIDTH)
    return tok(o_f[..., :half]), tok(o_b[..., half:2 * half])


def _merge_kernel(x_ref, mod_ref, gates_ref, *refs, alpha, n_prompt_blocks):
    p_refs, s_refs = refs[0:6], refs[6:12]
    gng_ref, gnb_ref, ones_ref, wup_ref, wout_ref, lng_ref, lnb_ref, o_ref = refs[12:]
    in_prompt = pl.program_id(0) < n_prompt_blocks
    oa, ob, sf, sb, g, bonus = [jnp.where(in_prompt, p[...], s[...]) for p, s in zip(p_refs, s_refs)]
    ones = ones_ref[...]
    inv_n = 1.0 / RWKV_HEAD_DIM
    o = sf + sb
    mu = _seg_sum(o, ones) * inv_n
    d = o - mu
    var = _seg_sum(d * d, ones) * inv_n
    oc = d * lax.rsqrt(var + RWKV_GN_EPS) * gng_ref[...] + gnb_ref[...]
    oc = (oc + bonus) * g
    branches = (oa, ob, oc)
    mix = None
    for i in range(N_BRANCH):
        up = _dot(branches[i].astype(BF16), wup_ref[i])
        term = _sigmoid(gates_ref[:, i * D_MODEL:(i + 1) * D_MODEL]) * up
        mix = term if mix is None else mix + term
    h = _dot(mix.astype(BF16), wout_ref[...])
    x = x_ref[...]
    y = alpha * x + mod_ref[0, 5:6, :] * h
    o_ref[...] = _layer_norm(y, lng_ref[...], lnb_ref[...])


def _merge_sublayer(x, mod, gates, prompt_parts, sample_parts, gn_g, gn_b, ones_bd, w_up, w_out,
                    ln_g, ln_b, *, alpha, row_map):
    n = x.shape[0]
    tm = ROW_TILE
    bw = BRANCH_WIDTH
    npb = prompt_parts[0].shape[0] // tm
    rows = lambda w: pl.BlockSpec((tm, w), lambda i: (i, 0))
    prow = pl.BlockSpec((tm, bw), lambda i: (jnp.minimum(i, npb - 1), 0))
    srow = pl.BlockSpec((tm, bw), lambda i: (jnp.maximum(i - npb, 0), 0))
    const = lambda shp: pl.BlockSpec(shp, lambda i: (0,) * len(shp))
    return pl.pallas_call(
        functools.partial(_merge_kernel, alpha=alpha, n_prompt_blocks=npb),
        out_shape=jax.ShapeDtypeStruct((n, D_MODEL), F32),
        grid=(n // tm,),
        in_specs=[rows(D_MODEL), pl.BlockSpec((1, N_MOD, D_MODEL), lambda i: (row_map(i), 0, 0)),
                  rows(N_BRANCH * D_MODEL)] + [prow] * 6 + [srow] * 6 + [
                  const((1, bw)), const((1, bw)), const((bw, bw)),
                  const((N_BRANCH, bw, D_MODEL)), const((D_MODEL, D_MODEL)),
                  const((1, D_MODEL)), const((1, D_MODEL))],
        out_specs=rows(D_MODEL),
        compiler_params=_cparams("arbitrary"),
        name="merge_sublayer",
    )(x, mod, gates, *prompt_parts, *sample_parts, gn_g, gn_b, ones_bd, w_up, w_out,
      ln_g.reshape(1, D_MODEL), ln_b.reshape(1, D_MODEL))


def _heads_first(x, B, T, H):
    return x.reshape(B, T, H, x.shape[-1] // H).transpose(0, 2, 1, 3)


def kernel(x_prompt, x_sample, cache_attn_k, cache_attn_v, cache_diff_k, cache_diff_v, state_rwkv, c, c_ctx,
           w_mod, b_mod, ln_g, ln_b, ffn_w_in, ffn_w_out, w_in, qk_norm_g, diff_lambda, diff_subln_g,
           rwkv_mu, rwkv_w0, rwkv_w2, rwkv_a0, rwkv_a2, rwkv_g2, rwkv_k_k, rwkv_k_a, rwkv_r_k,
           rwkv_gn_g, rwkv_gn_b, w_up, w_out):
    Bp, Tp, _ = x_prompt.shape
    Bs, Ts, _ = x_sample.shape
    depth = w_mod.shape[0]
    alpha = (2.0 * depth) ** 0.25
    n_p, n_s = Bp * Tp, Bs * Ts
    assert n_p % ROW_TILE == 0 and Ts % ROW_TILE == 0 and 1 + Bs <= MOD_ROWS
    row_map = _mod_row_map(n_p // ROW_TILE, Ts // ROW_TILE)

    x = jnp.concatenate([x_prompt.reshape(n_p, D_MODEL), x_sample.reshape(n_s, D_MODEL)], axis=0)
    cvecs = jnp.concatenate([c_ctx[None], c, jnp.zeros((MOD_ROWS - 1 - Bs, D_MODEL), F32)], axis=0)
    mod_all = _modulation(cvecs, w_mod, b_mod)

    seg = np.arange(BRANCH_WIDTH) // HEAD_DIM
    ones_bd = jnp.asarray(seg[:, None] == seg[None, :], BF16)
    rope = _rope_tables(Ts)
    bw = BRANCH_WIDTH

    n_ak, n_av, n_dk, n_dv, n_st = [], [], [], [], []
    for l in range(depth):
        lam_init = 0.8 - 0.6 * math.exp(-0.3 * l)
        mod = mod_all[l]
        x = _ffn_sublayer(x, mod, ffn_w_in[l, 0].astype(BF16), ffn_w_out[l, 0].astype(BF16),
                          ln_g[l, 0], ln_b[l, 0], sub=0, alpha=alpha, row_map=row_map)
        gates, aq, ak, av, bq, bk, bv, cin = _in_projection(x, mod, w_in[l].astype(BF16), row_map=row_map)

        gq = jnp.tile(qk_norm_g[l, 0], ATTN_HEADS)[None]
        gk = jnp.tile(qk_norm_g[l, 1], ATTN_KV_HEADS)[None]
        oa_p, kn_p = _attn_a(aq, ak, av, gq, gk, ones_bd, B=Bp, T=Tp, row0=0, latent=False)
        oa_s, _ = _attn_a(aq, ak, av, gq, gk, ones_bd, B=Bs, T=Ts, row0=n_p, latent=True, rope=rope,
                          ctx_k=cache_attn_k[:, l], ctx_v=cache_attn_v[:, l])
        sg = diff_subln_g[l][None]
        ob_p = _attn_b(bq, bk, bv, diff_lambda[l], sg, B=Bp, T=Tp, row0=0, latent=False, lam_init=lam_init)
        ob_s = _attn_b(bq, bk, bv, diff_lambda[l], sg, B=Bs, T=Ts, row0=n_p, latent=True, lam_init=lam_init,
                       rope=rope, ctx_k=cache_diff_k[:, l], ctx_v=cache_diff_v[:, l])

        rp = {'mu': rwkv_mu[l][None], 'w0': rwkv_w0[l], 'w2': rwkv_w2[l].astype(BF16), 'a0': rwkv_a0[l],
              'a2': rwkv_a2[l].astype(BF16), 'g2': rwkv_g2[l].astype(BF16),
              'k_a': rwkv_k_a[l][None], 'r_k': rwkv_r_k[l].reshape(1, bw)}
        r, k, v, w, a, g_p, bonus_p = _rwkv_prep(cin, rp, ones_bd, row0=0, n_rows=n_p, t_seq=Tp)
        sf_p, sb_p, st = _rwkv_context(r, k, v, w, a, rwkv_k_k[l], rwkv_k_a[l], Bp, Tp)
        r, k, v, w, a, g_s, bonus_s = _rwkv_prep(cin, rp, ones_bd, row0=n_p, n_rows=n_s, t_seq=Ts)
        sf_s, sb_s = _rwkv_latent(r, k, v, w, a, rwkv_k_k[l], rwkv_k_a[l], state_rwkv[:, l], Bs, Ts)
        n_st.append(st)
        x = _merge_sublayer(x, mod, gates, (oa_p, ob_p, sf_p, sb_p, g_p, bonus_p),
                            (oa_s, ob_s, sf_s, sb_s, g_s, bonus_s),
                            rwkv_gn_g[l][None], rwkv_gn_b[l][None], ones_bd,
                            w_up[l].astype(BF16), w_out[l].astype(BF16), ln_g[l, 1], ln_b[l, 1],
                            alpha=alpha, row_map=row_map)
        x = _ffn_sublayer(x, mod, ffn_w_in[l, 1].astype(BF16), ffn_w_out[l, 1].astype(BF16),
                          ln_g[l, 2], ln_b[l, 2], sub=2, alpha=alpha, row_map=row_map)

        n_ak.append(_heads_first(kn_p, Bp, Tp, ATTN_KV_HEADS))
        n_av.append(_heads_first(av[:n_p], Bp, Tp, ATTN_KV_HEADS))
        n_dk.append(_heads_first(bk[:n_p], Bp, Tp, DIFF_HEADS))
        n_dv.append(_heads_first(bv[:n_p], Bp, Tp, DIFF_HEADS))

    y_prompt = x[:n_p].reshape(Bp, Tp, D_MODEL)
    y_sample = x[n_p:].reshape(Bs, Ts, D_MODEL)
    stack = lambda parts: jnp.stack(parts, axis=1)
    return (y_prompt, y_sample, stack(n_ak), stack(n_av), stack(n_dk), stack(n_dv), stack(n_st))
```

```python
import functools
import math

import numpy as np
import jax
import jax.numpy as jnp
from jax import lax
from jax.experimental import pallas as pl
from jax.experimental.pallas import tpu as pltpu

F32 = jnp.float32
BF16 = jnp.bfloat16

D_MODEL = 1024
GRID_W = 64
HEAD_DIM = 64
ATTN_HEADS = 8
ATTN_KV_HEADS = 2
ATTN_GROUP = ATTN_HEADS // ATTN_KV_HEADS
DIFF_HEADS = 4
RWKV_HEADS = 8
RWKV_HEAD_DIM = 64
BRANCH_WIDTH = 512
W_LORA = 64
A_LORA = 64
G_LORA = 128
D_FF = 2816
N_BRANCH = 3
N_MOD = 9
ROPE_THETA = 10000.0
ROPE_AXIS_DIM = HEAD_DIM // 2
LN_EPS = 1e-5
RMS_EPS = 1e-6
RWKV_GN_EPS = 64e-5
A_Q_W = ATTN_HEADS * HEAD_DIM
A_KV_W = ATTN_KV_HEADS * HEAD_DIM
B_QK_W = DIFF_HEADS * 2 * HEAD_DIM
B_V_W = DIFF_HEADS * 2 * HEAD_DIM
C_IN_W = 3 * BRANCH_WIDTH + 2 * W_LORA + 2 * A_LORA + G_LORA
IN_SPLITS = (N_BRANCH * D_MODEL, A_Q_W, A_KV_W, A_KV_W, B_QK_W, B_QK_W, B_V_W, C_IN_W)
IN_WIDTH = sum(IN_SPLITS)

LANES = 128
SUBLANES = 8
VMEM_LIMIT_BYTES = 56 * 1024 * 1024

MOD_ROWS = 8
MOD_TN = 1152
FFN_CK = 256
ROW_TILE = 256
Q_TILE = 256
SCAN_TC = 32
SCAN_UNROLL = 32


def _cparams(*sem):
    return pltpu.CompilerParams(dimension_semantics=sem, vmem_limit_bytes=VMEM_LIMIT_BYTES)


def _sigmoid(x):
    return 1.0 / (1.0 + jnp.exp(-x))


def _dot(a, b):
    return jnp.dot(a, b, preferred_element_type=F32)


def _dot_t(a, b):
    return lax.dot_general(a, b, (((1,), (1,)), ((), ())), preferred_element_type=F32)


def _seg_sum(x, ones_bd):
    hi = x.astype(BF16)
    r1 = x - hi.astype(F32)
    mid = r1.astype(BF16)
    lo = (r1 - mid.astype(F32)).astype(BF16)
    return _dot(hi, ones_bd) + _dot(mid, ones_bd) + _dot(lo, ones_bd)


def _layer_norm(y, g, b):
    mu = jnp.mean(y, axis=-1, keepdims=True)
    d = y - mu
    var = jnp.mean(d * d, axis=-1, keepdims=True)
    return d * lax.rsqrt(var + LN_EPS) * g + b


def _mod_row_map(n_prompt_blocks, blocks_per_sample):
    def row(i):
        return jnp.where(i < n_prompt_blocks, 0, 1 + (i - n_prompt_blocks) // blocks_per_sample)
    return row


def _mod_kernel(c_ref, w_ref, b_ref, o_ref):
    cv = c_ref[...]
    h = (cv * _sigmoid(cv)).astype(BF16)
    o_ref[0] = _dot(h, w_ref[0].astype(BF16)) + b_ref[0]


def _modulation(cvecs, w_mod, b_mod):
    L = w_mod.shape[0]
    n = N_MOD * D_MODEL
    out = pl.pallas_call(
        _mod_kernel,
        out_shape=jax.ShapeDtypeStruct((L, MOD_ROWS, n), F32),
        grid=(L, n // MOD_TN),
        in_specs=[
            pl.BlockSpec((MOD_ROWS, D_MODEL), lambda l, j: (0, 0)),
            pl.BlockSpec((1, D_MODEL, MOD_TN), lambda l, j: (l, 0, j)),
            pl.BlockSpec((1, 1, MOD_TN), lambda l, j: (l, 0, j)),
        ],
        out_specs=pl.BlockSpec((1, MOD_ROWS, MOD_TN), lambda l, j: (l, 0, j)),
        compiler_params=_cparams("arbitrary", "arbitrary"),
        name="modulation",
    )(cvecs, w_mod, b_mod.reshape(L, 1, n))
    return out.reshape(L, MOD_ROWS, N_MOD, D_MODEL)


def _ffn_kernel(x_ref, mod_ref, win_ref, wout_ref, g_ref, b_ref, o_ref, act_ref, *, sub, alpha):
    x = x_ref[...]
    shift = mod_ref[0, 3 * sub:3 * sub + 1, :]
    scale = mod_ref[0, 3 * sub + 1:3 * sub + 2, :]
    gate = mod_ref[0, 3 * sub + 2:3 * sub + 3, :]
    h = (x * (1.0 + scale) + shift).astype(BF16)
    for c in range(D_FF // FFN_CK):
        a = _dot(h, win_ref[:, c * FFN_CK:(c + 1) * FFN_CK])
        b = _dot(h, win_ref[:, D_FF + c * FFN_CK:D_FF + (c + 1) * FFN_CK])
        act_ref[:, c * FFN_CK:(c + 1) * FFN_CK] = (a * _sigmoid(a) * b).astype(BF16)
    f = _dot(act_ref[...], wout_ref[...])
    y = alpha * x + (0.5 * gate) * f
    o_ref[...] = _layer_norm(y, g_ref[...], b_ref[...])


def _ffn_sublayer(x, mod, w_in, w_out, ln_g, ln_b, *, sub, alpha, row_map):
    n = x.shape[0]
    tm = ROW_TILE
    const = lambda i: (0, 0)
    return pl.pallas_call(
        functools.partial(_ffn_kernel, sub=sub, alpha=alpha),
        out_shape=jax.ShapeDtypeStruct((n, D_MODEL), F32),
        grid=(n // tm,),
        in_specs=[
            pl.BlockSpec((tm, D_MODEL), lambda i: (i, 0)),
            pl.BlockSpec((1, N_MOD, D_MODEL), lambda i: (row_map(i), 0, 0)),
            pl.BlockSpec((D_MODEL, 2 * D_FF), const),
            pl.BlockSpec((D_FF, D_MODEL), const),
            pl.BlockSpec((1, D_MODEL), const),
            pl.BlockSpec((1, D_MODEL), const),
        ],
        out_specs=pl.BlockSpec((tm, D_MODEL), lambda i: (i, 0)),
        scratch_shapes=[pltpu.VMEM((tm, D_FF), BF16)],
        compiler_params=_cparams("arbitrary"),
        name="ffn_sublayer",
    )(x, mod, w_in, w_out, ln_g.reshape(1, D_MODEL), ln_b.reshape(1, D_MODEL))


def _inproj_kernel(x_ref, mod_ref, w_ref, *out_refs):
    x = x_ref[...]
    shift = mod_ref[0, 3:4, :]
    scale = mod_ref[0, 4:5, :]
    h = (x * (1.0 + scale) + shift).astype(BF16)
    off = 0
    for o_ref, width in zip(out_refs, IN_SPLITS):
        o_ref[...] = _dot(h, w_ref[:, off:off + width])
        off += width


def _in_projection(x, mod, w_in, *, row_map):
    n = x.shape[0]
    tm = ROW_TILE
    return pl.pallas_call(
        _inproj_kernel,
        out_shape=tuple(jax.ShapeDtypeStruct((n, w), F32) for w in IN_SPLITS),
        grid=(n // tm,),
        in_specs=[
            pl.BlockSpec((tm, D_MODEL), lambda i: (i, 0)),
            pl.BlockSpec((1, N_MOD, D_MODEL), lambda i: (row_map(i), 0, 0)),
            pl.BlockSpec((D_MODEL, IN_WIDTH), lambda i: (0, 0)),
        ],
        out_specs=tuple(pl.BlockSpec((tm, w), lambda i: (i, 0)) for w in IN_SPLITS),
        compiler_params=_cparams("arbitrary"),
        name="in_projection",
    )(x, mod, w_in)


def _rope_tables(T):
    n_rows = T // GRID_W
    row = np.repeat(np.arange(n_rows), GRID_W).astype(np.float32)
    col = np.tile(np.arange(GRID_W), n_rows).astype(np.float32)
    inv = (1.0 / (ROPE_THETA ** (np.arange(0, ROPE_AXIS_DIM, 2, dtype=np.float32) / ROPE_AXIS_DIM))).astype(np.float32)
    ang_r = row[:, None] * inv
    ang_c = col[:, None] * inv
    z = np.zeros_like(ang_r)
    cos = np.concatenate([np.cos(ang_r), np.cos(ang_r), np.cos(ang_c), np.cos(ang_c)], axis=1)
    s_up = np.concatenate([-np.sin(ang_r), z, -np.sin(ang_c), z], axis=1)
    s_dn = np.concatenate([z, np.sin(ang_r), z, np.sin(ang_c)], axis=1)
    tile2 = lambda t: jnp.asarray(np.concatenate([t, t], axis=1), F32)
    return tile2(cos), tile2(s_up), tile2(s_dn)


def _rope128(x, cos, s_up, s_dn):
    up = pltpu.roll(x, LANES - ROPE_AXIS_DIM // 2, axis=1)
    dn = pltpu.roll(x, ROPE_AXIS_DIM // 2, axis=1)
    return x * cos + up * s_up + dn * s_dn


def _softmax_rows(s):
    m = jnp.max(s, axis=-1, keepdims=True)
    e = jnp.exp(s - m)
    return e / jnp.sum(e, axis=-1, keepdims=True)


def _attn_a_kernel(*refs, latent, T, past):
    o_ref, kn_ref, q_s, k_s, v_s = refs[-5:]
    if latent:
        aq_ref, ak_ref, av_ref, gq_ref, gk_ref, ones_ref, cos_ref, sup_ref, sdn_ref, ck_ref, cv_ref = refs[:11]
    else:
        aq_ref, ak_ref, av_ref, gq_ref, gk_ref, ones_ref = refs[:6]
    ones = ones_ref[...]
    inv_d = 1.0 / HEAD_DIM
    aq = aq_ref[...]
    qn = aq * lax.rsqrt(_seg_sum(aq * aq, ones) * inv_d + RMS_EPS) * gq_ref[...]
    ak = ak_ref[...]
    kn = ak * lax.rsqrt(_seg_sum(ak * ak, ones[:A_KV_W, :A_KV_W]) * inv_d + RMS_EPS) * gk_ref[...]
    kn_ref[...] = kn
    if latent:
        cos, sup, sdn = cos_ref[...], sup_ref[...], sdn_ref[...]
        for j in range(A_Q_W // LANES):
            q_s[:, j * LANES:(j + 1) * LANES] = _rope128(
                qn[:, j * LANES:(j + 1) * LANES], cos, sup, sdn).astype(BF16)
        kr = _rope128(kn, cos, sup, sdn).astype(BF16)
    else:
        q_s[...] = qn.astype(BF16)
        kr = kn.astype(BF16)
    av = av_ref[...].astype(BF16)
    for h in range(ATTN_KV_HEADS):
        k_s[h, 0:T, :] = kr[:, h * HEAD_DIM:(h + 1) * HEAD_DIM]
        v_s[h, 0:T, :] = av[:, h * HEAD_DIM:(h + 1) * HEAD_DIM]
        if latent:
            k_s[h, T:T + past, :] = ck_ref[0, h].astype(BF16)
            v_s[h, T:T + past, :] = cv_ref[0, h].astype(BF16)
    scale = HEAD_DIM ** -0.5
    for qb in range(T // Q_TILE):
        rows = slice(qb * Q_TILE, (qb + 1) * Q_TILE)
        outs = []
        for j in range(ATTN_HEADS):
            h = j // ATTN_GROUP
            q = q_s[rows, j * HEAD_DIM:(j + 1) * HEAD_DIM]
            s = _dot_t(q, k_s[h]) * scale
            p = _softmax_rows(s).astype(BF16)
            outs.append(_dot(p, v_s[h]))
        o_ref[rows, :] = jnp.concatenate(outs, axis=-1)


def _attn_a(aq, ak, av, gq, gk, ones_bd, *, B, T, row0, latent, rope=None, ctx_k=None, ctx_v=None):
    blk0 = row0 // T
    past = ctx_k.shape[2] if latent else 0
    rows = lambda w: pl.BlockSpec((T, w), lambda b: (blk0 + b, 0))
    const = lambda shp: pl.BlockSpec(shp, lambda b: (0,) * len(shp))
    in_specs = [rows(A_Q_W), rows(A_KV_W), rows(A_KV_W), const((1, A_Q_W)), const((1, A_KV_W)),
                const((A_Q_W, A_Q_W))]
    args = [aq, ak, av, gq, gk, ones_bd]
    if latent:
        in_specs += [const((T, LANES))] * 3
        in_specs += [pl.BlockSpec((1, ATTN_KV_HEADS, past, HEAD_DIM), lambda b: (b, 0, 0, 0))] * 2
        args += list(rope) + [ctx_k, ctx_v]
    return pl.pallas_call(
        functools.partial(_attn_a_kernel, latent=latent, T=T, past=past),
        out_shape=(jax.ShapeDtypeStruct((B * T, A_Q_W), F32), jax.ShapeDtypeStruct((B * T, A_KV_W), F32)),
        grid=(B,),
        in_specs=in_specs,
        out_specs=(pl.BlockSpec((T, A_Q_W), lambda b: (b, 0)), pl.BlockSpec((T, A_KV_W), lambda b: (b, 0))),
        scratch_shapes=[pltpu.VMEM((T, A_Q_W), BF16),
                        pltpu.VMEM((ATTN_KV_HEADS, T + past, HEAD_DIM), BF16),
                        pltpu.VMEM((ATTN_KV_HEADS, T + past, HEAD_DIM), BF16)],
        compiler_params=_cparams("arbitrary"),
        name="attn_gqa_latent" if latent else "attn_gqa_context",
    )(*args)


def _attn_b_kernel(*refs, latent, T, past, lam_init):
    o_ref, k_s, v_s = refs[-3:]
    if latent:
        bq_ref, bk_ref, bv_ref, lv_ref, sg_ref, cos_ref, sup_ref, sdn_ref, ck_ref, cv_ref = refs[:10]
    else:
        bq_ref, bk_ref, bv_ref, lv_ref, sg_ref = refs[:5]
    lv = lv_ref[...]
    d01 = jnp.sum(lv[0:1, :] * lv[1:2, :], axis=-1, keepdims=True)
    d23 = jnp.sum(lv[2:3, :] * lv[3:4, :], axis=-1, keepdims=True)
    lam = jnp.exp(d01) - jnp.exp(d23) + lam_init
    q = bq_ref[...]
    k = bk_ref[...]
    if latent:
        cos, sup, sdn = cos_ref[...], sup_ref[...], sdn_ref[...]
        q = _rope128(q, cos, sup, sdn)
        k = _rope128(k, cos, sup, sdn)
    q = q.astype(BF16)
    k_s[0:T, :] = k.astype(BF16)
    v_s[0:T, :] = bv_ref[...].astype(BF16)
    if latent:
        k_s[T:T + past, :] = ck_ref[0, 0].astype(BF16)
        v_s[T:T + past, :] = cv_ref[0, 0].astype(BF16)
    scale = HEAD_DIM ** -0.5
    inv_d = 1.0 / (2 * HEAD_DIM)
    k1 = k_s[:, 0:HEAD_DIM]
    k2 = k_s[:, HEAD_DIM:2 * HEAD_DIM]
    v = v_s[...]
    for qb in range(T // Q_TILE):
        rows = slice(qb * Q_TILE, (qb + 1) * Q_TILE)
        s1 = _dot_t(q[rows, 0:HEAD_DIM], k1) * scale
        s2 = _dot_t(q[rows, HEAD_DIM:2 * HEAD_DIM], k2) * scale
        p = (_softmax_rows(s1) - lam * _softmax_rows(s2)).astype(BF16)
        o = _dot(p, v)
        ms = jnp.mean(o * o, axis=-1, keepdims=True)
        o_ref[rows, :] = o * lax.rsqrt(ms + RMS_EPS) * sg_ref[...] * (1.0 - lam_init)


def _attn_b(bq, bk, bv, lam_vec, subln_g, *, B, T, row0, latent, lam_init, rope=None, ctx_k=None, ctx_v=None):
    blk0 = row0 // T
    past = ctx_k.shape[2] if latent else 0
    dh = 2 * HEAD_DIM
    rows = pl.BlockSpec((T, dh), lambda b, h: (blk0 + b, h))
    const = lambda shp: pl.BlockSpec(shp, lambda b, h: (0,) * len(shp))
    in_specs = [rows, rows, rows, const((4, HEAD_DIM)), const((1, dh))]
    args = [bq, bk, bv, lam_vec, subln_g]
    if latent:
        in_specs += [const((T, LANES))] * 3
        in_specs += [pl.BlockSpec((1, 1, past, dh), lambda b, h: (b, h, 0, 0))] * 2
        args += list(rope) + [ctx_k, ctx_v]
    return pl.pallas_call(
        functools.partial(_attn_b_kernel, latent=latent, T=T, past=past, lam_init=lam_init),
        out_shape=jax.ShapeDtypeStruct((B * T, B_V_W), F32),
        grid=(B, DIFF_HEADS),
        in_specs=in_specs,
        out_specs=pl.BlockSpec((T, dh), lambda b, h: (b, h)),
        scratch_shapes=[pltpu.VMEM((T + past, dh), BF16), pltpu.VMEM((T + past, dh), BF16)],
        compiler_params=_cparams("arbitrary", "arbitrary"),
        name="attn_diff_latent" if latent else "attn_diff_context",
    )(*args)


def _rwkv_prep_kernel(cin_ref, prev_ref, next_ref, mu_ref, w0_ref, w2_ref, a0_ref, a2_ref, g2_ref,
                      ka_ref, rk_ref, ones_ref, *rest, tm, per_seq):
    r_o, k_o, v_o, w_o, a_o, g_o, bonus_o = rest[-7:]
    pos = pl.program_id(0) % per_seq
    row_before = jnp.where(pos == 0, 0.0, prev_ref[SUBLANES - 1:SUBLANES, :])
    row_after = jnp.where(pos == per_seq - 1, 0.0, next_ref[0:1, :])
    x = cin_ref[...]
    t_idx = lax.broadcasted_iota(jnp.int32, (tm, 1), 0)
    prev = jnp.where(t_idx == 0, row_before, pltpu.roll(x, 1, axis=0))
    nxt = jnp.where(t_idx == tm - 1, row_after, pltpu.roll(x, tm - 1, axis=0))
    x = x + mu_ref[...] * (0.5 * (prev + nxt) - x)
    bw = BRANCH_WIDTH
    r = x[:, 0:bw]
    k = x[:, bw:2 * bw]
    v = x[:, 2 * bw:3 * bw]
    off = 3 * bw
    wl = x[:, off:off + 2 * W_LORA]
    al = x[:, off + 2 * W_LORA:off + 2 * W_LORA + 2 * A_LORA]
    gl = x[:, off + 2 * W_LORA + 2 * A_LORA:]
    ones = ones_ref[...]
    r_o[...] = r
    k_o[...] = k
    v_o[...] = v
    g_o[...] = _dot(_sigmoid(gl).astype(BF16), g2_ref[...])
    tw = jnp.tanh(wl).astype(BF16)
    alb = al.astype(BF16)
    decay_rate = math.exp(-0.5)
    bonus = jnp.zeros_like(r)
    for d in range(2):
        w_logit = w0_ref[d:d + 1, :] + _dot(tw[:, d * W_LORA:(d + 1) * W_LORA], w2_ref[d])
        w_o[d] = jnp.exp(-decay_rate * _sigmoid(w_logit))
        a = _sigmoid(a0_ref[d:d + 1, :] + _dot(alb[:, d * A_LORA:(d + 1) * A_LORA], a2_ref[d]))
        a_o[d] = a
        kc = k * (1.0 + (a - 1.0) * ka_ref[...])
        bonus = bonus + _seg_sum(r * kc * rk_ref[...], ones) * v
    bonus_o[...] = bonus


def _rwkv_prep(cin, p, ones_bd, *, row0, n_rows, t_seq):
    n = cin.shape[0]
    tm = ROW_TILE
    bw = BRANCH_WIDTH
    blk0 = row0 // tm
    halo = tm // SUBLANES
    last = n // SUBLANES - 1
    const = lambda shp: pl.BlockSpec(shp, lambda i: (0,) * len(shp))
    tok = jax.ShapeDtypeStruct((n_rows, bw), F32)
    tok2 = jax.ShapeDtypeStruct((2, n_rows, bw), F32)
    tspec = pl.BlockSpec((tm, bw), lambda i: (i, 0))
    tspec2 = pl.BlockSpec((2, tm, bw), lambda i: (0, i, 0))
    in_specs = [pl.BlockSpec((tm, C_IN_W), lambda i: (blk0 + i, 0)),
                pl.BlockSpec((SUBLANES, C_IN_W), lambda i: (jnp.maximum((blk0 + i) * halo - 1, 0), 0)),
                pl.BlockSpec((SUBLANES, C_IN_W), lambda i: (jnp.minimum((blk0 + i + 1) * halo, last), 0)),
                const((1, C_IN_W)), const((2, bw)), const((2, W_LORA, bw)), const((2, bw)),
                const((2, A_LORA, bw)), const((G_LORA, bw)), const((1, bw)), const((1, bw)),
                const((bw, bw))]
    args = [cin, cin, cin, p['mu'], p['w0'], p['w2'], p['a0'], p['a2'], p['g2'], p['k_a'], p['r_k'], ones_bd]
    return pl.pallas_call(
        functools.partial(_rwkv_prep_kernel, tm=tm, per_seq=t_seq // tm),
        out_shape=(tok, tok, tok, tok2, tok2, tok, tok),
        grid=(n_rows // tm,),
        in_specs=in_specs,
        out_specs=(tspec, tspec, tspec, tspec2, tspec2, tspec, tspec),
        compiler_params=_cparams("arbitrary"),
        name="rwkv_streams",
    )(*args)


def _scan_kernel(*refs, tc, nv, dir_lanes):
    if dir_lanes:
        (rf, kf, vf, wf, af, rb, kb, vb, wb, ab, bwd_ref, kk_ref, ka_ref, s0_ref,
         of_ref, ob_ref, sT_ref, S, al_s, wr_s, be_s, kc_s, w_s, v_s, c_s) = refs
        bwd = bwd_ref[...] > 0.0
    else:
        (rf, kf, vf, wf, af, kk_ref, ka_ref, s0_ref, of_ref, sT_ref,
         S, al_s, wr_s, be_s, kc_s, w_s, v_s, c_s) = refs
    d = pl.program_id(0)
    j = pl.program_id(2)
    n = RWKV_HEAD_DIM

    @pl.when(j == 0)
    def _():
        S[...] = s0_ref[...]

    def derive(i, carry):
        if dir_lanes:
            ib = tc - 1 - i
            r = jnp.where(bwd, rb[ib], rf[i])
            k = jnp.where(bwd, kb[ib], kf[i])
            v = jnp.where(bwd, vb[ib], vf[i])
            w = jnp.where(bwd, wb[ib], wf[i])
            a = jnp.where(bwd, ab[ib], af[i])
        else:
            row = i + d * (tc - 1 - 2 * i)
            r, k, v, w, a = rf[row], kf[row], vf[row], wf[row], af[row]
        kkv = k * kk_ref[...]
        kk = kkv * lax.rsqrt(jnp.sum(kkv * kkv, axis=0, keepdims=True) + 1e-12)
        kc = k * (1.0 + (a - 1.0) * ka_ref[...])
        beta = kk * a
        al_s[i] = -kk
        wr_s[i] = w * r
        be_s[i] = beta
        kc_s[i] = kc
        w_s[i] = w
        v_s[i] = v
        c_s[i, 0:1, :] = jnp.sum(beta * r, axis=0, keepdims=True)
        c_s[i, 1:2, :] = jnp.sum(kc * r, axis=0, keepdims=True)
        return carry

    lax.fori_loop(0, tc, derive, 0)

    def step(i, carry):
        def body_a(kx, acc):
            sa, y = acc
            sk = S[kx]
            return sa + sk * al_s[i, pl.ds(kx, 1), :], y + sk * wr_s[i, pl.ds(kx, 1), :]

        zero = jnp.zeros((nv, LANES), F32)
        sa, y = lax.fori_loop(0, n, body_a, (zero, zero), unroll=SCAN_UNROLL)
        v = v_s[i]

        def body_b(kx, c):
            S[kx] = (S[kx] * w_s[i, pl.ds(kx, 1), :] + be_s[i, pl.ds(kx, 1), :] * sa
                     + kc_s[i, pl.ds(kx, 1), :] * v)
            return c

        lax.fori_loop(0, n, body_b, 0, unroll=SCAN_UNROLL)
        out = y + sa * c_s[i, 0:1, :] + v * c_s[i, 1:2, :]
        if dir_lanes:
            of_ref[i] = out
            ob_ref[tc - 1 - i] = out
        else:
            of_ref[i + d * (tc - 1 - 2 * i)] = out
        return carry

    lax.fori_loop(0, tc, step, 0)

    @pl.when(j == pl.num_programs(2) - 1)
    def _():
        sT_ref[...] = S[...]


def _scan_scratch(nv):
    n = RWKV_HEAD_DIM
    tc = SCAN_TC
    return ([pltpu.VMEM((n, nv, LANES), F32)] + [pltpu.VMEM((tc, n, LANES), F32)] * 5
            + [pltpu.VMEM((tc, nv, LANES), F32), pltpu.VMEM((tc, SUBLANES, LANES), F32)])


def _rwkv_scan_groups(r, k, v, w, a, kk_t, ka_t, s0):
    LG, T = r.shape[:2]
    n = RWKV_HEAD_DIM
    tc = SCAN_TC
    nt = T // tc
    tb = lambda d, j: j + d * (nt - 1 - 2 * j)
    shared = pl.BlockSpec((None, tc, n, LANES), lambda d, g, j: (g, tb(d, j), 0, 0))
    perdir = pl.BlockSpec((None, None, tc, n, LANES), lambda d, g, j: (d, g, tb(d, j), 0, 0))
    par = pl.BlockSpec((n, LANES), lambda d, g, j: (0, 0))
    st = pl.BlockSpec((None, None, n, n, LANES), lambda d, g, j: (d, g, 0, 0, 0))
    return pl.pallas_call(
        functools.partial(_scan_kernel, tc=tc, nv=n, dir_lanes=False),
        out_shape=(jax.ShapeDtypeStruct((2, LG, T, n, LANES), F32), jax.ShapeDtypeStruct((2, LG, n, n, LANES), F32)),
        grid=(2, LG, nt),
        in_specs=[shared, shared, shared, perdir, perdir, par, par, st],
        out_specs=(perdir, st),
        scratch_shapes=_scan_scratch(n),
        compiler_params=_cparams("arbitrary", "arbitrary", "arbitrary"),
        name="rwkv_scan_context",
    )(r, k, v, w, a, kk_t, ka_t, s0)


def _rwkv_scan_lanes(r, k, v, w, a, bwd, kk_t, ka_t, s0):
    T = r.shape[0]
    nv = v.shape[1]
    n = RWKV_HEAD_DIM
    tc = SCAN_TC
    nt = T // tc
    fwd_k = pl.BlockSpec((tc, n, LANES), lambda d, g, j: (j, 0, 0))
    bwd_k = pl.BlockSpec((tc, n, LANES), lambda d, g, j: (nt - 1 - j, 0, 0))
    fwd_v = pl.BlockSpec((tc, nv, LANES), lambda d, g, j: (j, 0, 0))
    bwd_v = pl.BlockSpec((tc, nv, LANES), lambda d, g, j: (nt - 1 - j, 0, 0))
    par = pl.BlockSpec((n, LANES), lambda d, g, j: (0, 0))
    st = pl.BlockSpec((n, nv, LANES), lambda d, g, j: (0, 0, 0))
    o_sds = jax.ShapeDtypeStruct((T, nv, LANES), F32)
    return pl.pallas_call(
        functools.partial(_scan_kernel, tc=tc, nv=nv, dir_lanes=True),
        out_shape=(o_sds, o_sds, jax.ShapeDtypeStruct((n, nv, LANES), F32)),
        grid=(1, 1, nt),
        in_specs=[fwd_k, fwd_k, fwd_v, fwd_k, fwd_k, bwd_k, bwd_k, bwd_v, bwd_k, bwd_k,
                  pl.BlockSpec((1, LANES), lambda d, g, j: (0, 0)), par, par, st],
        out_specs=(fwd_v, bwd_v, st),
        scratch_shapes=_scan_scratch(nv),
        compiler_params=_cparams("arbitrary", "arbitrary", "arbitrary"),
        name="rwkv_scan_latent",
    )(r, k, v, w, a, r, k, v, w, a, bwd, kk_t, ka_t, s0)


def _to_lanes(x, B, T):
    lead = x.shape[:-2]
    x = x.reshape(lead + (B, T, RWKV_HEADS, RWKV_HEAD_DIM))
    nl = len(lead)
    perm = tuple(range(nl)) + (nl + 1, nl + 3, nl + 0, nl + 2)
    return x.transpose(perm).reshape(lead + (T, RWKV_HEAD_DIM, B * RWKV_HEADS))


def _lane_param(p, vl):
    t = jnp.repeat(p.reshape(RWKV_HEADS, RWKV_HEAD_DIM).T, vl, axis=1)
    return jnp.tile(t, (1, LANES // (RWKV_HEADS * vl)))


def _rwkv_context(r, k, v, w, a, k_k, k_a, B, T):
    n = RWKV_HEAD_DIM
    nb = B * RWKV_HEADS
    LG = -(-nb // LANES)

    def groups(x):
        f = _to_lanes(x, B, T)
        f = jnp.pad(f, [(0, 0)] * (f.ndim - 1) + [(0, LG * LANES - nb)])
        f = f.reshape(f.shape[:-1] + (LG, LANES))
        return jnp.moveaxis(f, -2, -4)

    s0 = jnp.zeros((2, LG, n, n, LANES), F32)
    o, sT = _rwkv_scan_groups(groups(r), groups(k), groups(v), groups(w), groups(a),
                              _lane_param(k_k, 1), _lane_param(k_a, 1), s0)
    o = jnp.moveaxis(o, 1, 3).reshape(2, T, n, LG * LANES)[..., :nb]
    o = o.reshape(2, T, n, B, RWKV_HEADS).transpose(0, 3, 1, 4, 2).reshape(2, B * T, BRANCH_WIDTH)
    s = jnp.moveaxis(sT, 1, 3).reshape(2, n, n, LG * LANES)[..., :nb]
    s = s.reshape(2, n, n, B, RWKV_HEADS).transpose(3, 0, 4, 2, 1)
    return o[0], o[1], s


def _rwkv_latent(r, k, v, w, a, k_k, k_a, state0, B, T):
    n = RWKV_HEAD_DIM
    H = RWKV_HEADS
    vl = min(SUBLANES, LANES // (2 * B * H))
    assert vl >= 1 and vl & (vl - 1) == 0
    nv = n // vl
    half = B * H * vl
    pad = lambda x: jnp.pad(x, [(0, 0)] * (x.ndim - 1) + [(0, LANES - 2 * half)])

    def k_lanes(x):
        return jnp.repeat(_to_lanes(x, B, T), vl, axis=-1)

    def shared_k(x):
        f = k_lanes(x)
        return pad(jnp.concatenate([f, f], axis=-1))

    def per_dir(x):
        f = k_lanes(x)
        return pad(jnp.concatenate([f[0], f[1]], axis=-1))

    vv = v.reshape(B, T, H, vl, nv).transpose(1, 4, 0, 2, 3).reshape(T, nv, half)
    vv = pad(jnp.concatenate([vv, vv], axis=-1))
    s0 = state0.reshape(B, 2, H, vl, nv, n).transpose(5, 4, 1, 0, 2, 3).reshape(n, nv, 2 * half)
    bwd = (jnp.arange(LANES) >= half).astype(F32)[None]
    o_f, o_b, _ = _rwkv_scan_lanes(shared_k(r), shared_k(k), vv, per_dir(w), per_dir(a), bwd,
                                   _lane_param(k_k, vl), _lane_param(k_a, vl), pad(s0))
    tok = lambda o: o.reshape(T, nv, B, H, vl).transpose(2, 0, 3, 4, 1).reshape(B * T, BRANCH_WIDTH)
    return tok(o_f[..., :half]), tok(o_b[..., half:2 * half])


def _merge_kernel(x_ref, mod_ref, gates_ref, *refs, alpha, n_prompt_blocks):
    p_refs, s_refs = refs[0:6], refs[6:12]
    gng_ref, gnb_ref, ones_ref, wup_ref, wout_ref, lng_ref, lnb_ref, o_ref = refs[12:]
    in_prompt = pl.program_id(0) < n_prompt_blocks
    oa, ob, sf, sb, g, bonus = [jnp.where(in_prompt, p[...], s[...]) for p, s in zip(p_refs, s_refs)]
    ones = ones_ref[...]
    inv_n = 1.0 / RWKV_HEAD_DIM
    o = sf + sb
    mu = _seg_sum(o, ones) * inv_n
    d = o - mu
    var = _seg_sum(d * d, ones) * inv_n
    oc = d * lax.rsqrt(var + RWKV_GN_EPS) * gng_ref[...] + gnb_ref[...]
    oc = (oc + bonus) * g
    branches = (oa, ob, oc)
    mix = None
    for i in range(N_BRANCH):
        up = _dot(branches[i].astype(BF16), wup_ref[i])
        term = _sigmoid(gates_ref[:, i * D_MODEL:(i + 1) * D_MODEL]) * up
        mix = term if mix is None else mix + term
    h = _dot(mix.astype(BF16), wout_ref[...])
    x = x_ref[...]
    y = alpha * x + mod_ref[0, 5:6, :] * h
    o_ref[...] = _layer_norm(y, lng_ref[...], lnb_ref[...])


def _merge_sublayer(x, mod, gates, prompt_parts, sample_parts, gn_g, gn_b, ones_bd, w_up, w_out,
                    ln_g, ln_b, *, alpha, row_map):
    n = x.shape[0]
    tm = ROW_TILE
    bw = BRANCH_WIDTH
    npb = prompt_parts[0].shape[0] // tm
    rows = lambda w: pl.BlockSpec((tm, w), lambda i: (i, 0))
    prow = pl.BlockSpec((tm, bw), lambda i: (jnp.minimum(i, npb - 1), 0))
    srow = pl.BlockSpec((tm, bw), lambda i: (jnp.maximum(i - npb, 0), 0))
    const = lambda shp: pl.BlockSpec(shp, lambda i: (0,) * len(shp))
    return pl.pallas_call(
        functools.partial(_merge_kernel, alpha=alpha, n_prompt_blocks=npb),
        out_shape=jax.ShapeDtypeStruct((n, D_MODEL), F32),
        grid=(n // tm,),
        in_specs=[rows(D_MODEL), pl.BlockSpec((1, N_MOD, D_MODEL), lambda i: (row_map(i), 0, 0)),
                  rows(N_BRANCH * D_MODEL)] + [prow] * 6 + [srow] * 6 + [
                  const((1, bw)), const((1, bw)), const((bw, bw)),
                  const((N_BRANCH, bw, D_MODEL)), const((D_MODEL, D_MODEL)),
                  const((1, D_MODEL)), const((1, D_MODEL))],
        out_specs=rows(D_MODEL),
        compiler_params=_cparams("arbitrary"),
        name="merge_sublayer",
    )(x, mod, gates, *prompt_parts, *sample_parts, gn_g, gn_b, ones_bd, w_up, w_out,
      ln_g.reshape(1, D_MODEL), ln_b.reshape(1, D_MODEL))


def _heads_first(x, B, T, H):
    return x.reshape(B, T, H, x.shape[-1] // H).transpose(0, 2, 1, 3)


def kernel(x_prompt, x_sample, cache_attn_k, cache_attn_v, cache_diff_k, cache_diff_v, state_rwkv, c, c_ctx,
           w_mod, b_mod, ln_g, ln_b, ffn_w_in, ffn_w_out, w_in, qk_norm_g, diff_lambda, diff_subln_g,
           rwkv_mu, rwkv_w0, rwkv_w2, rwkv_a0, rwkv_a2, rwkv_g2, rwkv_k_k, rwkv_k_a, rwkv_r_k,
           rwkv_gn_g, rwkv_gn_b, w_up, w_out):
    Bp, Tp, _ = x_prompt.shape
    Bs, Ts, _ = x_sample.shape
    depth = w_mod.shape[0]
    alpha = (2.0 * depth) ** 0.25
    n_p, n_s = Bp * Tp, Bs * Ts
    assert n_p % ROW_TILE == 0 and Ts % ROW_TILE == 0 and 1 + Bs <= MOD_ROWS
    row_map = _mod_row_map(n_p // ROW_TILE, Ts // ROW_TILE)

    x = jnp.concatenate([x_prompt.reshape(n_p, D_MODEL), x_sample.reshape(n_s, D_MODEL)], axis=0)
    cvecs = jnp.concatenate([c_ctx[None], c, jnp.zeros((MOD_ROWS - 1 - Bs, D_MODEL), F32)], axis=0)
    mod_all = _modulation(cvecs, w_mod, b_mod)

    seg = np.arange(BRANCH_WIDTH) // HEAD_DIM
    ones_bd = jnp.asarray(seg[:, None] == seg[None, :], BF16)
    rope = _rope_tables(Ts)
    bw = BRANCH_WIDTH

    n_ak, n_av, n_dk, n_dv, n_st = [], [], [], [], []
    for l in range(depth):
        lam_init = 0.8 - 0.6 * math.exp(-0.3 * l)
        mod = mod_all[l]
        x = _ffn_sublayer(x, mod, ffn_w_in[l, 0].astype(BF16), ffn_w_out[l, 0].astype(BF16),
                          ln_g[l, 0], ln_b[l, 0], sub=0, alpha=alpha, row_map=row_map)
        gates, aq, ak, av, bq, bk, bv, cin = _in_projection(x, mod, w_in[l].astype(BF16), row_map=row_map)

        gq = jnp.tile(qk_norm_g[l, 0], ATTN_HEADS)[None]
        gk = jnp.tile(qk_norm_g[l, 1], ATTN_KV_HEADS)[None]
        oa_p, kn_p = _attn_a(aq, ak, av, gq, gk, ones_bd, B=Bp, T=Tp, row0=0, latent=False)
        oa_s, _ = _attn_a(aq, ak, av, gq, gk, ones_bd, B=Bs, T=Ts, row0=n_p, latent=True, rope=rope,
                          ctx_k=cache_attn_k[:, l], ctx_v=cache_attn_v[:, l])
        sg = diff_subln_g[l][None]
        ob_p = _attn_b(bq, bk, bv, diff_lambda[l], sg, B=Bp, T=Tp, row0=0, latent=False, lam_init=lam_init)
        ob_s = _attn_b(bq, bk, bv, diff_lambda[l], sg, B=Bs, T=Ts, row0=n_p, latent=True, lam_init=lam_init,
                       rope=rope, ctx_k=cache_diff_k[:, l], ctx_v=cache_diff_v[:, l])

        rp = {'mu': rwkv_mu[l][None], 'w0': rwkv_w0[l], 'w2': rwkv_w2[l].astype(BF16), 'a0': rwkv_a0[l],
              'a2': rwkv_a2[l].astype(BF16), 'g2': rwkv_g2[l].astype(BF16),
              'k_a': rwkv_k_a[l][None], 'r_k': rwkv_r_k[l].reshape(1, bw)}
        r, k, v, w, a, g_p, bonus_p = _rwkv_prep(cin, rp, ones_bd, row0=0, n_rows=n_p, t_seq=Tp)
        sf_p, sb_p, st = _rwkv_context(r, k, v, w, a, rwkv_k_k[l], rwkv_k_a[l], Bp, Tp)
        r, k, v, w, a, g_s, bonus_s = _rwkv_prep(cin, rp, ones_bd, row0=n_p, n_rows=n_s, t_seq=Ts)
        sf_s, sb_s = _rwkv_latent(r, k, v, w, a, rwkv_k_k[l], rwkv_k_a[l], state_rwkv[:, l], Bs, Ts)
        n_st.append(st)
        x = _merge_sublayer(x, mod, gates, (oa_p, ob_p, sf_p, sb_p, g_p, bonus_p),
                            (oa_s, ob_s, sf_s, sb_s, g_s, bonus_s),
                            rwkv_gn_g[l][None], rwkv_gn_b[l][None], ones_bd,
                            w_up[l].astype(BF16), w_out[l].astype(BF16), ln_g[l, 1], ln_b[l, 1],
                            alpha=alpha, row_map=row_map)
        x = _ffn_sublayer(x, mod, ffn_w_in[l, 1].astype(BF16), ffn_w_out[l, 1].astype(BF16),
                          ln_g[l, 2], ln_b[l, 2], sub=2, alpha=alpha, row_map=row_map)

        n_ak.append(_heads_first(kn_p, Bp, Tp, ATTN_KV_HEADS))
        n_av.append(_heads_first(av[:n_p], Bp, Tp, ATTN_KV_HEADS))
        n_dk.append(_heads_first(bk[:n_p], Bp, Tp, DIFF_HEADS))
        n_dv.append(_heads_first(bv[:n_p], Bp, Tp, DIFF_HEADS))

    y_prompt = x[:n_p].reshape(Bp, Tp, D_MODEL)
    y_sample = x[n_p:].reshape(Bs, Ts, D_MODEL)
    stack = lambda parts: jnp.stack(parts, axis=1)
    return (y_prompt, y_sample, stack(n_ak), stack(n_av), stack(n_dk), stack(n_dv), stack(n_st))
```
